```python
import math
import jax, jax.numpy as jnp
from jax import lax
import numpy as np

D_MODEL = 2048
BATCH = 2
SEQ = 16384
DEPTH = 1

HEAD_DIM = 128
N_HEADS_A = 8
N_HEADS_B = 8
MIX_WIDTH = (N_HEADS_A + N_HEADS_B) * HEAD_DIM
KV_RANK = 256
IDX_HEADS = 16
IDX_DIM = 64
TOPK_MAX = 256
N_BUCKETS = 32
MAX_DISTANCE = 128
D_FF = 5632
CONV_WIDTH = 3
Q_BLOCK = 128
EPS = 1e-6

COL_SIZES = (
    N_HEADS_A * HEAD_DIM,
    KV_RANK,
    IDX_HEADS * IDX_DIM,
    IDX_DIM,
    IDX_HEADS,
    N_HEADS_B * HEAD_DIM,
    N_HEADS_B * HEAD_DIM,
    N_HEADS_B * HEAD_DIM,
    N_HEADS_B,
)
D_IN = sum(COL_SIZES)

kernel_name = 'hymba_dsa_fox_convffn_layer'


def rmsnorm(x, g):
    xf = x.astype(jnp.float32)
    y = xf * lax.rsqrt(jnp.mean(xf * xf, axis=-1, keepdims=True) + EPS)
    return (y * g.astype(jnp.float32)).astype(x.dtype)


def t5_bucket(n):
    max_exact = N_BUCKETS // 2
    nf = jnp.maximum(n, 1).astype(jnp.float32)
    large = max_exact + (jnp.log(nf / max_exact) / math.log(MAX_DISTANCE / max_exact)
                         * (N_BUCKETS - max_exact)).astype(jnp.int32)
    large = jnp.minimum(large, N_BUCKETS - 1)
    return jnp.where(n < max_exact, n, large)


def to_blocks(a):
    b, l = a.shape[0], a.shape[1]
    a = a.reshape((b, l // Q_BLOCK, Q_BLOCK) + a.shape[2:])
    return jnp.moveaxis(a, 1, 0)


def from_blocks(a):
    a = jnp.moveaxis(a, 0, 1)
    return a.reshape((a.shape[0], a.shape[1] * a.shape[2]) + a.shape[3:])


def dsa_mixer(q, c_kv, q_idx, k_idx, w_idx, w_uk, w_uv, t5_table):
    L = q.shape[1]
    nb = L // Q_BLOCK
    topk = min(TOPK_MAX, L // 4)
    key_pos = jnp.arange(L)
    q_lat = jnp.einsum('bthd,hdr->bthr', q, w_uk)

    def block(args):
        blk, ql, qi, wi = args
        t_pos = blk * Q_BLOCK + jnp.arange(Q_BLOCK)
        causal = key_pos[None, :] <= t_pos[:, None]
        dots = jnp.einsum('bthd,bsd->bths', qi, k_idx).astype(jnp.float32) * (IDX_DIM ** -0.5)
        score = jnp.einsum('bth,bths->bts', wi.astype(jnp.float32) * (IDX_HEADS ** -0.5),
                           jax.nn.relu(dots))
        score = jnp.where(causal[None], score, -jnp.inf)
        _, sel = lax.top_k(score, topk)
        c_sel = jax.vmap(lambda c, i: c[i])(c_kv, sel)
        dist = t_pos[None, :, None] - sel
        valid = dist >= 0
        bias = t5_table[t5_bucket(jnp.maximum(dist, 0))]
        logits = (jnp.einsum('bthr,btkr->bthk', ql, c_sel).astype(jnp.float32) * (HEAD_DIM ** -0.5)
                  + jnp.moveaxis(bias, -1, 2).astype(jnp.float32))
        logits = jnp.where(valid[:, :, None, :], logits, -jnp.inf)
        p = jax.nn.softmax(logits, axis=-1).astype(c_sel.dtype)
        return jnp.einsum('bthk,btkr->bthr', p, c_sel)

    o_lat = from_blocks(lax.map(block, (jnp.arange(nb), to_blocks(q_lat),
                                        to_blocks(q_idx), to_blocks(w_idx))))
    return jnp.einsum('bthr,hrd->bthd', o_lat, w_uv)


def fox_mixer(q, k, v, log_f):
    L = q.shape[1]
    nb = L // Q_BLOCK
    key_pos = jnp.arange(L)
    cum = jnp.cumsum(log_f.astype(jnp.float32), axis=1)
    cum_k = jnp.moveaxis(cum, 1, 2)

    def block(args):
        blk, qb, cq = args
        t_pos = blk * Q_BLOCK + jnp.arange(Q_BLOCK)
        causal = key_pos[None, :] <= t_pos[:, None]
        decay = jnp.moveaxis(cq, 1, 2)[..., None] - cum_k[:, :, None, :]
        logits = jnp.einsum('bthd,bshd->bhts', qb, k).astype(jnp.float32) * (HEAD_DIM ** -0.5) + decay
        logits = jnp.where(causal[None, None], logits, -jnp.inf)
        p = jax.nn.softmax(logits, axis=-1).astype(v.dtype)
        return jnp.einsum('bhts,bshd->bthd', p, v)

    return from_blocks(lax.map(block, (jnp.arange(nb), to_blocks(q), to_blocks(cum))))


def conv_ffn(h, w_up, conv_w, conv_b, w_down):
    L = h.shape[1]
    u = h @ w_up
    up = jnp.pad(u, ((0, 0), (CONV_WIDTH - 1, 0), (0, 0)))
    acc = conv_b
    for j in range(CONV_WIDTH):
        acc = acc + conv_w[j] * up[:, j:j + L]
    gate, val = jnp.split(acc, 2, axis=-1)
    return (jax.nn.silu(gate) * val) @ w_down


def setup_inputs(seed: int = 0) -> dict:
    key = jax.random.key(seed)
    ks = jax.random.split(key, 15)
    f32 = jnp.float32
    n = lambda k, s: jax.random.normal(k, s, f32)
    return {
        'x': n(ks[0], (BATCH, SEQ, D_MODEL)),
        'w_in': n(ks[1], (DEPTH, D_MODEL, D_IN)) * D_MODEL ** -0.5,
        'kv_norm_g': 1.0 + 0.02 * n(ks[2], (DEPTH, KV_RANK)),
        'w_uk': n(ks[3], (DEPTH, N_HEADS_A, HEAD_DIM, KV_RANK)) * KV_RANK ** -0.5,
        'w_uv': n(ks[4], (DEPTH, N_HEADS_A, KV_RANK, HEAD_DIM)) * KV_RANK ** -0.5,
        't5_table': 0.5 * n(ks[5], (N_BUCKETS, N_HEADS_A)),
        'fgate_b': 3.0 + 0.1 * n(ks[6], (DEPTH, N_HEADS_B)),
        'w_o': n(ks[7], (DEPTH, MIX_WIDTH, D_MODEL)) * MIX_WIDTH ** -0.5,
        'norm1_g': 1.0 + 0.02 * n(ks[8], (DEPTH, D_MODEL)),
        'norm2_g': 1.0 + 0.02 * n(ks[9], (DEPTH, D_MODEL)),
        'w_up': n(ks[10], (DEPTH, D_MODEL, 2 * D_FF)) * D_MODEL ** -0.5,
        'conv_w': 0.5 * n(ks[11], (DEPTH, CONV_WIDTH, 2 * D_FF)),
        'conv_b': 0.02 * n(ks[12], (DEPTH, 2 * D_FF)),
        'w_down': n(ks[13], (DEPTH, D_FF, D_MODEL)) * D_FF ** -0.5,
        'final_g': 1.0 + 0.02 * n(ks[14], (D_MODEL,)),
    }


def reference(x, w_in, kv_norm_g, w_uk, w_uv, t5_table, fgate_b, w_o, norm1_g, norm2_g,
              w_up, conv_w, conv_b, w_down, final_g):
    B, L, _ = x.shape
    split_points = [int(s) for s in np.cumsum(COL_SIZES)[:-1]]
    for l in range(DEPTH):
        h = rmsnorm(x, norm1_g[l])
        proj = h @ w_in[l]
        aq, akv, aiq, aik, aiw, bq, bk, bv, bf = jnp.split(proj, split_points, axis=-1)
        c_kv = rmsnorm(akv, kv_norm_g[l])
        o_a = dsa_mixer(aq.reshape(B, L, N_HEADS_A, HEAD_DIM), c_kv,
                        aiq.reshape(B, L, IDX_HEADS, IDX_DIM), aik, aiw,
                        w_uk[l], w_uv[l], t5_table)
        log_f = jax.nn.log_sigmoid(bf.astype(jnp.float32) + fgate_b[l].astype(jnp.float32))
        o_b = fox_mixer(bq.reshape(B, L, N_HEADS_B, HEAD_DIM),
                        bk.reshape(B, L, N_HEADS_B, HEAD_DIM),
                        bv.reshape(B, L, N_HEADS_B, HEAD_DIM), log_f)
        mix = jnp.concatenate([o_a.reshape(B, L, -1), o_b.reshape(B, L, -1)], axis=-1)
        x = x + mix @ w_o[l]
        x = x + conv_ffn(rmsnorm(x, norm2_g[l]), w_up[l], conv_w[l], conv_b[l], w_down[l])
    return rmsnorm(x, final_g)
```

```python
import functools
import math

import numpy as np
import jax
import jax.numpy as jnp
from jax import lax
from jax.experimental import pallas as pl
from jax.experimental.pallas import tpu as pltpu

HEAD_DIM = 128
N_HEADS_A = 8
N_HEADS_B = 8
KV_RANK = 256
IDX_HEADS = 16
IDX_DIM = 64
TOPK_MAX = 256
N_BUCKETS = 32
MAX_DISTANCE = 128
CONV_WIDTH = 3
EPS = 1e-6

F32 = jnp.float32
BF16 = jnp.bfloat16
INT_MIN = -2 ** 31
NEG = -1e30

VMEM_LIMIT_BYTES = 56 * 1024 * 1024

DSA_TQ = 128
DSA_TK = 256
IDX_HEAD_GROUP = 4


def _cparams(sem):
    return pltpu.CompilerParams(dimension_semantics=sem, vmem_limit_bytes=VMEM_LIMIT_BYTES)


def _resident(block_shape, index_map):
    return pl.BlockSpec(block_shape, index_map, pipeline_mode=pl.Buffered(1))


def _rms(x, g):
    ms = jnp.mean(x * x, axis=-1, keepdims=True)
    return x * lax.rsqrt(ms + EPS) * g


def _norm_proj_kernel(x_ref, g_ref, w_ref, o_ref, *, n_chunk):
    h = _rms(x_ref[...], g_ref[...]).astype(BF16)
    n = o_ref.shape[1]
    for c0 in range(0, n, n_chunk):
        c1 = min(c0 + n_chunk, n)
        o_ref[:, c0:c1] = jnp.dot(h, w_ref[:, c0:c1], preferred_element_type=F32).astype(o_ref.dtype)


def _norm_proj_latent_kernel(x_ref, g_ref, w_ref, kvg_ref, o_ref, c_ref, *, n_chunk):
    h = _rms(x_ref[...], g_ref[...]).astype(BF16)
    n = o_ref.shape[1]
    for c0 in range(0, n, n_chunk):
        c1 = min(c0 + n_chunk, n)
        o_ref[:, c0:c1] = jnp.dot(h, w_ref[:, c0:c1], preferred_element_type=F32)
    c_ref[...] = _rms(o_ref[:, 0:KV_RANK], kvg_ref[...]).astype(c_ref.dtype)


def norm_proj(x, g, w, out_dtype, tm, kv_g=None):
    B, L, D = x.shape
    N = w.shape[1]
    tm = min(tm, L)
    grid = (B, L // tm)
    x_spec = pl.BlockSpec((None, tm, D), lambda b, i: (b, i, 0))
    g_spec = pl.BlockSpec((1, D), lambda b, i: (0, 0))
    w_spec = _resident((D, N), lambda b, i: (0, 0))
    o_spec = pl.BlockSpec((None, tm, N), lambda b, i: (b, i, 0))
    if kv_g is None:
        return pl.pallas_call(
            functools.partial(_norm_proj_kernel, n_chunk=512),
            grid=grid, in_specs=[x_spec, g_spec, w_spec], out_specs=o_spec,
            out_shape=jax.ShapeDtypeStruct((B, L, N), out_dtype),
            compiler_params=_cparams(("parallel", "parallel")), name="norm_proj",
        )(x, g.reshape(1, D), w)
    return pl.pallas_call(
        functools.partial(_norm_proj_latent_kernel, n_chunk=512),
        grid=grid,
        in_specs=[x_spec, g_spec, w_spec, pl.BlockSpec((1, KV_RANK), lambda b, i: (0, 0))],
        out_specs=[o_spec, pl.BlockSpec((None, tm, KV_RANK), lambda b, i: (b, i, 0))],
        out_shape=[jax.ShapeDtypeStruct((B, L, N), out_dtype),
                   jax.ShapeDtypeStruct((B, L, KV_RANK), BF16)],
        compiler_params=_cparams(("parallel", "parallel")), name="norm_proj_latent",
    )(x, g.reshape(1, D), w, kv_g.reshape(1, KV_RANK))


def _dsa_kernel(aq_ref, wuk_ref, qcat_ref, wt_ref, khl_ref, c_ref, ct_ref, bias_ref, wuv_ref,
                out_ref, s_ref, ql_ref, acc_ref, m_ref, l_ref, *, topk):
    tq, tk = DSA_TQ, DSA_TK
    i = pl.program_id(1)
    nkt = (i * tq) // tk + 1
    t_pos = i * tq + lax.broadcasted_iota(jnp.int32, (tk, tq), 1)
    s_loc = lax.broadcasted_iota(jnp.int32, (tk, tq), 0)

    scale = HEAD_DIM ** -0.5
    for h in range(N_HEADS_A):
        ql = jnp.dot(aq_ref[:, h * HEAD_DIM:(h + 1) * HEAD_DIM], wuk_ref[h], preferred_element_type=F32)
        ql_ref[h * tq:(h + 1) * tq, :] = (ql * scale).astype(BF16)

    def score_tile(j, carry):
        k0 = pl.multiple_of(j * tk, tk)
        khl = khl_ref[pl.ds(k0, tk), :]
        kcat = jnp.concatenate([khl, khl], axis=1)
        s = jnp.zeros((tk, tq), F32)
        hg = IDX_HEAD_GROUP
        for g in range(IDX_HEADS // hg):
            d = lax.dot_general(kcat, qcat_ref[g * hg * tq:(g + 1) * hg * tq, :],
                                (((1,), (1,)), ((), ())), preferred_element_type=F32)
            for hh in range(hg):
                h = g * hg + hh
                s = s + jnp.maximum(d[:, hh * tq:(hh + 1) * tq], 0.0) * wt_ref[h:h + 1, :]
        s = jnp.where(s == 0.0, 0.0, s)
        bits = lax.bitcast_convert_type(s, jnp.int32)
        key = bits ^ ((bits >> 31) & 0x7FFFFFFF)
        key = jnp.where(k0 + s_loc <= t_pos, key, INT_MIN)
        s_ref[pl.ds(k0, tk), :] = key
        return carry

    lax.fori_loop(0, nkt, score_tile, 0)

    def count_ge(cand):
        def body(j, acc):
            k0 = pl.multiple_of(j * tk, tk)
            ge = jnp.where(s_ref[pl.ds(k0, tk), :] >= cand, 1, 0).astype(jnp.int32)
            return acc + ge.reshape(tk // 8, 8, tq).sum(axis=0)
        acc = lax.fori_loop(0, nkt, body, jnp.zeros((8, tq), jnp.int32))
        return acc.sum(axis=0, keepdims=True)

    def bit_step(b, carry):
        thr, cnt_thr = carry
        cand = thr + lax.shift_left(jnp.int32(1), 31 - b)
        cnt = count_ge(cand)
        take = cnt >= topk
        return jnp.where(take, cand, thr), jnp.where(take, cnt, cnt_thr)

    thr0 = jnp.full((1, tq), INT_MIN, jnp.int32)
    cnt0 = jnp.zeros((1, tq), jnp.int32) + nkt * tk
    thr, cnt_thr = lax.fori_loop(0, 32, bit_step, (thr0, cnt0))

    @pl.when(jnp.max(cnt_thr) > topk)
    def _():
        need = (topk - count_ge(thr + 1)).astype(F32)
        tri = jnp.where(lax.broadcasted_iota(jnp.int32, (tk, tk), 0)
                        > lax.broadcasted_iota(jnp.int32, (tk, tk), 1), 1.0, 0.0).astype(BF16)

        def body(j, seen):
            k0 = pl.multiple_of(j * tk, tk)
            x = s_ref[pl.ds(k0, tk), :]
            eq = x == thr
            eqf = jnp.where(eq, 1.0, 0.0)
            rank = jnp.dot(tri, eqf.astype(BF16), preferred_element_type=F32) + seen
            s_ref[pl.ds(k0, tk), :] = jnp.where(eq & (rank >= need), INT_MIN, x)
            return seen + eqf.sum(axis=0, keepdims=True)

        lax.fori_loop(0, nkt, body, jnp.zeros((1, tq), F32))

    m_ref[...] = jnp.full(m_ref.shape, NEG, F32)
    l_ref[...] = jnp.zeros(l_ref.shape, F32)
    acc_ref[...] = jnp.zeros(acc_ref.shape, F32)

    def attend_tile(j, carry):
        k0 = pl.multiple_of(j * tk, tk)
        sel = (s_ref[pl.ds(k0, tk), :] >= thr) & (k0 + s_loc <= t_pos)
        lt = lax.dot_general(c_ref[pl.ds(k0, tk), :], ql_ref[...],
                             (((1,), (1,)), ((), ())), preferred_element_type=F32)
        variant = jnp.minimum((i * tq - k0) // 128, 3)
        lt = lt + bias_ref[variant]
        m_old = m_ref[...]
        ps, ms, sums = [], [], []
        for h in range(N_HEADS_A):
            z = jnp.where(sel, lt[:, h * tq:(h + 1) * tq], NEG)
            m_new = jnp.maximum(m_old[:, h * tq:(h + 1) * tq], z.max(axis=0, keepdims=True))
            p = jnp.exp(z - m_new)
            ms.append(m_new)
            sums.append(p.sum(axis=0, keepdims=True))
            ps.append(p.astype(BF16))
        m_new = jnp.concatenate(ms, axis=1)
        alpha = jnp.exp(m_old - m_new)
        l_ref[...] = alpha * l_ref[...] + jnp.concatenate(sums, axis=1)
        m_ref[...] = m_new
        pv = jnp.dot(ct_ref[:, pl.ds(k0, tk)], jnp.concatenate(ps, axis=1), preferred_element_type=F32)
        acc_ref[...] = alpha * acc_ref[...] + pv
        return carry

    lax.fori_loop(0, nkt, attend_tile, 0)

    o_t = acc_ref[...] / l_ref[...]
    for h in range(N_HEADS_A):
        o_h = o_t[:, h * tq:(h + 1) * tq].T.astype(BF16)
        out_ref[:, h * HEAD_DIM:(h + 1) * HEAD_DIM] = jnp.dot(
            o_h, wuv_ref[h], preferred_element_type=F32).astype(out_ref.dtype)


def _t5_bucket_np(n):
    max_exact = N_BUCKETS // 2
    nf = np.maximum(n, 1).astype(np.float32)
    large = max_exact + (np.log(nf / max_exact) / math.log(MAX_DISTANCE / max_exact)
                         * (N_BUCKETS - max_exact)).astype(np.int32)
    large = np.minimum(large, N_BUCKETS - 1)
    return np.where(n < max_exact, n, large)


def _dsa_bias_tiles(t5_table):
    tq, tk = DSA_TQ, DSA_TK
    assert MAX_DISTANCE == 128 and tq == 128 and tk == 256
    sl = np.arange(tk)[:, None]
    tl = np.arange(tq)[None, :]
    idx = np.stack([_t5_bucket_np(np.maximum(tl - sl + off, 0)) for off in (0, 128, 256)])
    far = t5_table[N_BUCKETS - 1].astype(F32)
    near = t5_table.astype(F32)[idx] - far
    near = jnp.transpose(near, (0, 1, 3, 2)).reshape(3, tk, N_HEADS_A * tq)
    return jnp.concatenate([near, jnp.zeros((1, tk, N_HEADS_A * tq), F32)], axis=0)


def dsa_mixer(aq, pf, c, w_uk, w_uv, t5_table):
    B, L, _ = aq.shape
    tq, tk = DSA_TQ, DSA_TK
    assert L % tk == 0
    nq = L // tq
    topk = min(TOPK_MAX, L // 4)
    assert tk >= topk
    o = KV_RANK
    q_idx = pf[..., o:o + IDX_HEADS * IDX_DIM]
    o += IDX_HEADS * IDX_DIM
    k_idx = pf[..., o:o + IDX_DIM]
    o += IDX_DIM
    w_idx = pf[..., o:o + IDX_HEADS]

    qi = q_idx.reshape(B, nq, tq, IDX_HEADS, IDX_DIM).transpose(0, 1, 3, 2, 4)
    qi = qi.reshape(B, nq, IDX_HEADS * tq, IDX_DIM)
    q_hi = qi.astype(BF16)
    q_lo = (qi - q_hi.astype(F32)).astype(BF16)
    qcat = jnp.concatenate([q_hi, q_hi, q_lo, q_lo], axis=-1)
    k_hi = k_idx.astype(BF16)
    k_lo = (k_idx - k_hi.astype(F32)).astype(BF16)
    khl = jnp.concatenate([k_hi, k_lo], axis=-1)
    wt = (w_idx * (IDX_DIM ** -0.5 * IDX_HEADS ** -0.5)).reshape(B, nq, tq, IDX_HEADS)
    wt = wt.transpose(0, 1, 3, 2)
    ct = jnp.swapaxes(c, 1, 2)
    bias = _dsa_bias_tiles(t5_table)

    hq = N_HEADS_A * tq
    return pl.pallas_call(
        functools.partial(_dsa_kernel, topk=topk),
        grid=(B, nq),
        in_specs=[
            pl.BlockSpec((None, tq, N_HEADS_A * HEAD_DIM), lambda b, i: (b, i, 0)),
            _resident((N_HEADS_A, HEAD_DIM, KV_RANK), lambda b, i: (0, 0, 0)),
            pl.BlockSpec((None, None, IDX_HEADS * tq, 4 * IDX_DIM), lambda b, i: (b, i, 0, 0)),
            pl.BlockSpec((None, None, IDX_HEADS, tq), lambda b, i: (b, i, 0, 0)),
            _resident((None, L, 2 * IDX_DIM), lambda b, i: (b, 0, 0)),
            _resident((None, L, KV_RANK), lambda b, i: (b, 0, 0)),
            _resident((None, KV_RANK, L), lambda b, i: (b, 0, 0)),
            _resident((4, tk, hq), lambda b, i: (0, 0, 0)),
            _resident((N_HEADS_A, KV_RANK, HEAD_DIM), lambda b, i: (0, 0, 0)),
        ],
        out_specs=pl.BlockSpec((None, tq, N_HEADS_A * HEAD_DIM), lambda b, i: (b, i, 0)),
        out_shape=jax.ShapeDtypeStruct((B, L, N_HEADS_A * HEAD_DIM), BF16),
        scratch_shapes=[
            pltpu.VMEM((L, tq), jnp.int32),
            pltpu.VMEM((hq, KV_RANK), BF16),
            pltpu.VMEM((KV_RANK, hq), F32),
            pltpu.VMEM((1, hq), F32),
            pltpu.VMEM((1, hq), F32),
        ],
        compiler_params=_cparams(("parallel", "arbitrary")), name="dsa",
    )(aq, w_uk.astype(BF16), qcat, wt, khl, c, ct, bias, w_uv.astype(BF16))


def _fox_kernel(q_ref, k_ref, v_ref, ck_ref, o_ref, m_ref, l_ref, acc_ref, *, tq, tk):
    i = pl.program_id(2)
    j = pl.program_id(3)

    @pl.when(j == 0)
    def _():
        m_ref[...] = jnp.full(m_ref.shape, NEG, F32)
        l_ref[...] = jnp.zeros(l_ref.shape, F32)
        acc_ref[...] = jnp.zeros(acc_ref.shape, F32)

    @pl.when(j * tk <= i * tq + tq - 1)
    def _():
        s = lax.dot_general(q_ref[...], k_ref[...], (((1,), (1,)), ((), ())), preferred_element_type=F32)
        s = s * (HEAD_DIM ** -0.5) - ck_ref[...]
        t_pos = i * tq + lax.broadcasted_iota(jnp.int32, (tq, tk), 0)
        s_pos = j * tk + lax.broadcasted_iota(jnp.int32, (tq, tk), 1)
        s = jnp.where(s_pos <= t_pos, s, NEG)
        m_old = m_ref[...]
        m_new = jnp.maximum(m_old, s.max(axis=1, keepdims=True))
        p = jnp.exp(s - m_new)
        alpha = jnp.exp(m_old - m_new)
        l_ref[...] = alpha * l_ref[...] + p.sum(axis=1, keepdims=True)
        acc_ref[...] = alpha * acc_ref[...] + jnp.dot(p.astype(BF16), v_ref[...], preferred_element_type=F32)
        m_ref[...] = m_new

    @pl.when(j == pl.num_programs(3) - 1)
    def _():
        o_ref[...] = (acc_ref[...] / l_ref[...]).astype(o_ref.dtype)


def fox_mixer(qkv, col0, ck, tq=512, tk=512):
    B, L, _ = qkv.shape
    tq = min(tq, L)
    tk = min(tk, L)
    H = N_HEADS_B

    def kv_idx(off):
        def f(b, h, i, j):
            return (b, jnp.minimum(j, (i * tq + tq - 1) // tk), col0 + off + h)
        return f

    return pl.pallas_call(
        functools.partial(_fox_kernel, tq=tq, tk=tk),
        grid=(B, H, L // tq, L // tk),
        in_specs=[
            pl.BlockSpec((None, tq, HEAD_DIM), lambda b, h, i, j: (b, i, col0 + h)),
            pl.BlockSpec((None, tk, HEAD_DIM), kv_idx(H)),
            pl.BlockSpec((None, tk, HEAD_DIM), kv_idx(2 * H)),
            pl.BlockSpec((None, None, 1, tk),
                         lambda b, h, i, j: (b, h, 0, jnp.minimum(j, (i * tq + tq - 1) // tk))),
        ],
        out_specs=pl.BlockSpec((None, tq, HEAD_DIM), lambda b, h, i, j: (b, i, h)),
        out_shape=jax.ShapeDtypeStruct((B, L, H * HEAD_DIM), BF16),
        scratch_shapes=[pltpu.VMEM((tq, 1), F32), pltpu.VMEM((tq, 1), F32), pltpu.VMEM((tq, HEAD_DIM), F32)],
        compiler_params=_cparams(("parallel", "parallel", "parallel", "arbitrary")), name="fox",
    )(qkv, qkv, qkv, ck)


def _out_proj_kernel(x_ref, oa_ref, ob_ref, wa_ref, wb_ref, g_ref, x2_ref, hn_ref):
    y = x_ref[...] + jnp.dot(oa_ref[...], wa_ref[...], preferred_element_type=F32)
    y = y + jnp.dot(ob_ref[...], wb_ref[...], preferred_element_type=F32)
    x2_ref[...] = y
    hn_ref[...] = _rms(y, g_ref[...]).astype(hn_ref.dtype)


def out_proj(x, oa, ob, w_o, g2, tm=512):
    B, L, D = x.shape
    tm = min(tm, L)
    wa = w_o[:N_HEADS_A * HEAD_DIM].astype(BF16)
    wb = w_o[N_HEADS_A * HEAD_DIM:].astype(BF16)
    row = lambda n: pl.BlockSpec((None, tm, n), lambda b, i: (b, i, 0))
    return pl.pallas_call(
        _out_proj_kernel,
        grid=(B, L // tm),
        in_specs=[row(D), row(oa.shape[-1]), row(ob.shape[-1]),
                  _resident(wa.shape, lambda b, i: (0, 0)), _resident(wb.shape, lambda b, i: (0, 0)),
                  pl.BlockSpec((1, D), lambda b, i: (0, 0))],
        out_specs=[row(D), row(D)],
        out_shape=[jax.ShapeDtypeStruct((B, L, D), F32), jax.ShapeDtypeStruct((B, L, D), BF16)],
        compiler_params=_cparams(("parallel", "parallel")), name="out_proj",
    )(x, oa, ob, wa, wb, g2.reshape(1, D))


FFN_HALO = 16


def _ffn_kernel(h_ref, halo_ref, x2_ref, wg_ref, wv_ref, cwg_ref, cwv_ref, cbg_ref, cbv_ref, wd_ref, fg_ref,
                o_ref, hbuf_ref, ug_ref, uv_ref, acc_ref, *, tm):
    i = pl.program_id(1)
    f = pl.program_id(2)

    @pl.when(f == 0)
    def _():
        hbuf_ref[0:FFN_HALO, :] = jnp.where(i > 0, halo_ref[...], jnp.zeros_like(halo_ref))
        hbuf_ref[FFN_HALO:, :] = h_ref[...]
        acc_ref[...] = jnp.zeros(acc_ref.shape, F32)

    ug_ref[...] = jnp.dot(hbuf_ref[...], wg_ref[...], preferred_element_type=F32)
    uv_ref[...] = jnp.dot(hbuf_ref[...], wv_ref[...], preferred_element_type=F32)

    def conv(u_ref, cw_ref, cb_ref):
        a = cb_ref[...] + cw_ref[0:1, :] * u_ref[pl.ds(FFN_HALO - 2, tm), :]
        a = a + cw_ref[1:2, :] * u_ref[pl.ds(FFN_HALO - 1, tm), :]
        return a + cw_ref[2:3, :] * u_ref[pl.ds(FFN_HALO, tm), :]

    gate = conv(ug_ref, cwg_ref, cbg_ref)
    val = conv(uv_ref, cwv_ref, cbv_ref)
    act = (gate * jax.nn.sigmoid(gate) * val).astype(BF16)
    acc_ref[...] += jnp.dot(act, wd_ref[...], preferred_element_type=F32)

    @pl.when(f == pl.num_programs(2) - 1)
    def _():
        o_ref[...] = _rms(x2_ref[...] + acc_ref[...], fg_ref[...]).astype(o_ref.dtype)


def conv_ffn_final(hn, x2, w_up, conv_w, conv_b, w_down, final_g, tm=512, fc=512):
    B, L, D = hn.shape
    F = w_down.shape[0]
    tm = min(tm, L)
    assert F % fc == 0 and tm % FFN_HALO == 0
    nf = F // fc
    wu = w_up.astype(BF16)
    wd = w_down.astype(BF16)
    cb = conv_b.reshape(1, 2 * F)
    hpt = tm // FFN_HALO
    return pl.pallas_call(
        functools.partial(_ffn_kernel, tm=tm),
        grid=(B, L // tm, nf),
        in_specs=[
            pl.BlockSpec((None, tm, D), lambda b, i, f: (b, i, 0)),
            pl.BlockSpec((None, FFN_HALO, D), lambda b, i, f: (b, jnp.maximum(i * hpt - 1, 0), 0)),
            pl.BlockSpec((None, tm, D), lambda b, i, f: (b, i, 0)),
            pl.BlockSpec((D, fc), lambda b, i, f: (0, f)),
            pl.BlockSpec((D, fc), lambda b, i, f: (0, nf + f)),
            pl.BlockSpec((CONV_WIDTH, fc), lambda b, i, f: (0, f)),
            pl.BlockSpec((CONV_WIDTH, fc), lambda b, i, f: (0, nf + f)),
            pl.BlockSpec((1, fc), lambda b, i, f: (0, f)),
            pl.BlockSpec((1, fc), lambda b, i, f: (0, nf + f)),
            pl.BlockSpec((fc, D), lambda b, i, f: (f, 0)),
            pl.BlockSpec((1, D), lambda b, i, f: (0, 0)),
        ],
        out_specs=pl.BlockSpec((None, tm, D), lambda b, i, f: (b, i, 0)),
        out_shape=jax.ShapeDtypeStruct((B, L, D), F32),
        scratch_shapes=[pltpu.VMEM((tm + FFN_HALO, D), BF16),
                        pltpu.VMEM((tm + FFN_HALO, fc), F32),
                        pltpu.VMEM((tm + FFN_HALO, fc), F32),
                        pltpu.VMEM((tm, D), F32)],
        compiler_params=_cparams(("parallel", "parallel", "arbitrary")), name="conv_ffn",
    )(hn, hn, x2, wu, wu, conv_w, conv_w, cb, cb, wd, final_g.reshape(1, D))


def kernel(x, w_in, kv_norm_g, w_uk, w_uv, t5_table, fgate_b, w_o, norm1_g, norm2_g,
           w_up, conv_w, conv_b, w_down, final_g):
    B, L, D = x.shape
    depth = w_in.shape[0]
    na = N_HEADS_A * HEAD_DIM
    nb = N_HEADS_B * HEAD_DIM
    sizes = (na, KV_RANK, IDX_HEADS * IDX_DIM, IDX_DIM, IDX_HEADS, nb, nb, nb, N_HEADS_B)
    offs = np.concatenate([[0], np.cumsum(sizes)])
    n_f32 = KV_RANK + IDX_HEADS * IDX_DIM + IDX_DIM + IDX_HEADS + N_HEADS_B
    n_f32_pad = -(-n_f32 // 128) * 128
    assert depth == 1, "the final-norm FFN epilogue closes a single-layer stack"
    for l in range(depth):
        w = w_in[l]
        w_attn = jnp.concatenate([w[:, offs[0]:offs[1]], w[:, offs[5]:offs[8]]], axis=1).astype(BF16)
        w_misc = jnp.concatenate([w[:, offs[1]:offs[5]], w[:, offs[8]:offs[9]],
                                  jnp.zeros((D, n_f32_pad - n_f32), w.dtype)], axis=1).astype(BF16)
        pa = norm_proj(x, norm1_g[l], w_attn, BF16, tm=512)
        pf, c = norm_proj(x, norm1_g[l], w_misc, F32, tm=512, kv_g=kv_norm_g[l])
        o_a = dsa_mixer(pa, pf, c, w_uk[l], w_uv[l], t5_table)
        bf = pf[..., n_f32 - N_HEADS_B:n_f32]
        log_f = jax.nn.log_sigmoid(bf + fgate_b[l].astype(F32))
        ck = jnp.cumsum(log_f, axis=1).transpose(0, 2, 1)[:, :, None, :]
        o_b = fox_mixer(pa, N_HEADS_A, ck)
        x2, hn = out_proj(x, o_a, o_b, w_o[l], norm2_g[l])
        return conv_ffn_final(hn, x2, w_up[l], conv_w[l], conv_b[l], w_down[l], final_g)
```

```python
import functools
import math

import numpy as np
import jax
import jax.numpy as jnp
from jax import lax
from jax.experimental import pallas as pl
from jax.experimental.pallas import tpu as pltpu

HEAD_DIM = 128
N_HEADS_A = 8
N_HEADS_B = 8
KV_RANK = 256
IDX_HEADS = 16
IDX_DIM = 64
TOPK_MAX = 256
N_BUCKETS = 32
MAX_DISTANCE = 128
CONV_WIDTH = 3
EPS = 1e-6

F32 = jnp.float32
BF16 = jnp.bfloat16
INT_MIN = -2 ** 31
NEG = -1e30
LOG2E = math.log2(math.e)

VMEM_LIMIT_BYTES = 56 * 1024 * 1024

DSA_TQ = 128
DSA_TK = 256
IDX_HEAD_GROUP = 4
DSA_SCORE_TILES = 2


def _cparams(sem):
    return pltpu.CompilerParams(dimension_semantics=sem, vmem_limit_bytes=VMEM_LIMIT_BYTES)


def _resident(block_shape, index_map):
    return pl.BlockSpec(block_shape, index_map, pipeline_mode=pl.Buffered(1))


def _rms(x, g):
    ms = jnp.mean(x * x, axis=-1, keepdims=True)
    return x * lax.rsqrt(ms + EPS) * g


def _norm_proj_kernel(x_ref, g_ref, w_ref, o_ref, *, n_chunk):
    h = _rms(x_ref[...], g_ref[...]).astype(BF16)
    n = o_ref.shape[1]
    for c0 in range(0, n, n_chunk):
        c1 = min(c0 + n_chunk, n)
        o_ref[:, c0:c1] = jnp.dot(h, w_ref[:, c0:c1], preferred_element_type=F32).astype(o_ref.dtype)


def _norm_proj_latent_kernel(x_ref, g_ref, w_ref, kvg_ref, o_ref, c_ref, *, n_chunk):
    h = _rms(x_ref[...], g_ref[...]).astype(BF16)
    n = o_ref.shape[1]
    for c0 in range(0, n, n_chunk):
        c1 = min(c0 + n_chunk, n)
        o_ref[:, c0:c1] = jnp.dot(h, w_ref[:, c0:c1], preferred_element_type=F32)
    c_ref[...] = _rms(o_ref[:, 0:KV_RANK], kvg_ref[...]).astype(c_ref.dtype)


def norm_proj(x, g, w, out_dtype, tm, kv_g=None):
    B, L, D = x.shape
    N = w.shape[1]
    tm = min(tm, L)
    grid = (B, L // tm)
    x_spec = pl.BlockSpec((None, tm, D), lambda b, i: (b, i, 0))
    g_spec = pl.BlockSpec((1, D), lambda b, i: (0, 0))
    w_spec = _resident((D, N), lambda b, i: (0, 0))
    o_spec = pl.BlockSpec((None, tm, N), lambda b, i: (b, i, 0))
    if kv_g is None:
        return pl.pallas_call(
            functools.partial(_norm_proj_kernel, n_chunk=512),
            grid=grid, in_specs=[x_spec, g_spec, w_spec], out_specs=o_spec,
            out_shape=jax.ShapeDtypeStruct((B, L, N), out_dtype),
            compiler_params=_cparams(("parallel", "parallel")), name="norm_proj",
        )(x, g.reshape(1, D), w)
    return pl.pallas_call(
        functools.partial(_norm_proj_latent_kernel, n_chunk=512),
        grid=grid,
        in_specs=[x_spec, g_spec, w_spec, pl.BlockSpec((1, KV_RANK), lambda b, i: (0, 0))],
        out_specs=[o_spec, pl.BlockSpec((None, tm, KV_RANK), lambda b, i: (b, i, 0))],
        out_shape=[jax.ShapeDtypeStruct((B, L, N), out_dtype),
                   jax.ShapeDtypeStruct((B, L, KV_RANK), BF16)],
        compiler_params=_cparams(("parallel", "parallel")), name="norm_proj_latent",
    )(x, g.reshape(1, D), w, kv_g.reshape(1, KV_RANK))


def _dsa_kernel(aq_ref, wuk_ref, qcat_ref, wt_ref, khl_ref, c_ref, ct_ref, bias_ref, wuv_ref,
                out_ref, s_ref, ql_ref, acc_ref, m_ref, l_ref, lta_ref, ltb_ref, *, topk):
    tq, tk = DSA_TQ, DSA_TK
    i = pl.program_id(1)
    nkt = (i * tq) // tk + 1
    t_pos = i * tq + lax.broadcasted_iota(jnp.int32, (tk, tq), 1)
    s_loc = lax.broadcasted_iota(jnp.int32, (tk, tq), 0)

    scale = HEAD_DIM ** -0.5 * LOG2E
    for h in range(N_HEADS_A):
        ql = jnp.dot(aq_ref[:, h * HEAD_DIM:(h + 1) * HEAD_DIM], wuk_ref[h], preferred_element_type=F32)
        ql_ref[h * tq:(h + 1) * tq, :] = (ql * scale).astype(BF16)

    sub = DSA_SCORE_TILES
    tc = sub * tk
    n_chunks = (i * tq) // tc + 1

    def score_chunk(jc, carry):
        for u in range(sub):
            k0 = pl.multiple_of(jc * tc + u * tk, tk)
            khl = khl_ref[pl.ds(k0, tk), :]
            kcat = jnp.concatenate([khl, khl], axis=1)
            s = jnp.zeros((tk, tq), F32)
            hg = IDX_HEAD_GROUP
            for g in range(IDX_HEADS // hg):
                d = lax.dot_general(kcat, qcat_ref[g * hg * tq:(g + 1) * hg * tq, :],
                                    (((1,), (1,)), ((), ())), preferred_element_type=F32)
                for hh in range(hg):
                    h = g * hg + hh
                    s = s + jnp.maximum(d[:, hh * tq:(hh + 1) * tq], 0.0) * wt_ref[h:h + 1, :]
            s = jnp.where(s == 0.0, 0.0, s)
            bits = lax.bitcast_convert_type(s, jnp.int32)
            key = bits ^ ((bits >> 31) & 0x7FFFFFFF)
            key = jnp.where(k0 + s_loc <= t_pos, key, INT_MIN)
            s_ref[pl.ds(k0, tk), :] = key
        return carry

    lax.fori_loop(0, n_chunks, score_chunk, 0)

    def count_ge(cand):
        def body(jc, acc):
            k0 = pl.multiple_of(jc * tc, tc)
            ge = jnp.where(s_ref[pl.ds(k0, tc), :] >= cand, 1, 0).astype(jnp.int32)
            return acc + ge.reshape(tc // 8, 8, tq).sum(axis=0)
        acc = lax.fori_loop(0, n_chunks, body, jnp.zeros((8, tq), jnp.int32))
        return acc.sum(axis=0, keepdims=True)

    def bit_step(b, carry):
        thr, cnt_thr = carry
        cand = thr + lax.shift_left(jnp.int32(1), 31 - b)
        cnt = count_ge(cand)
        take = cnt >= topk
        return jnp.where(take, cand, thr), jnp.where(take, cnt, cnt_thr)

    thr0 = jnp.full((1, tq), INT_MIN, jnp.int32)
    cnt0 = jnp.zeros((1, tq), jnp.int32) + n_chunks * tc
    thr, cnt_thr = lax.fori_loop(0, 32, bit_step, (thr0, cnt0))

    @pl.when(jnp.max(cnt_thr) > topk)
    def _():
        need = (topk - count_ge(thr + 1)).astype(F32)
        tri = jnp.where(lax.broadcasted_iota(jnp.int32, (tk, tk), 0)
                        > lax.broadcasted_iota(jnp.int32, (tk, tk), 1), 1.0, 0.0).astype(BF16)

        def body(j, seen):
            k0 = pl.multiple_of(j * tk, tk)
            x = s_ref[pl.ds(k0, tk), :]
            eq = x == thr
            eqf = jnp.where(eq, 1.0, 0.0)
            rank = jnp.dot(tri, eqf.astype(BF16), preferred_element_type=F32) + seen
            s_ref[pl.ds(k0, tk), :] = jnp.where(eq & (rank >= need), INT_MIN, x)
            return seen + eqf.sum(axis=0, keepdims=True)

        lax.fori_loop(0, nkt, body, jnp.zeros((1, tq), F32))

    m_ref[...] = jnp.full(m_ref.shape, NEG, F32)
    l_ref[...] = jnp.zeros(l_ref.shape, F32)
    acc_ref[...] = jnp.zeros(acc_ref.shape, F32)

    def logits(j):
        k0 = pl.multiple_of(j * tk, tk)
        return lax.dot_general(c_ref[pl.ds(k0, tk), :], ql_ref[...],
                               (((1,), (1,)), ((), ())), preferred_element_type=F32)

    def attend_tile(j, lt, near):
        k0 = pl.multiple_of(j * tk, tk)
        sel = s_ref[pl.ds(k0, tk), :] >= thr
        if near:
            sel = sel & (k0 + s_loc <= t_pos)
            variant = jnp.minimum((i * tq - k0) // MAX_DISTANCE, 2)
            lt = lt + bias_ref[variant]
        m_old = m_ref[...]
        ps, ms, sums = [], [], []
        for h in range(N_HEADS_A):
            z = jnp.where(sel, lt[:, h * tq:(h + 1) * tq], NEG)
            m_new = jnp.maximum(m_old[:, h * tq:(h + 1) * tq], z.max(axis=0, keepdims=True))
            p = jnp.exp2(z - m_new)
            ms.append(m_new)
            sums.append(p.sum(axis=0, keepdims=True))
            ps.append(p.astype(BF16))
        m_new = jnp.concatenate(ms, axis=1)
        alpha = jnp.exp2(m_old - m_new)
        l_ref[...] = alpha * l_ref[...] + jnp.concatenate(sums, axis=1)
        m_ref[...] = m_new
        pv = jnp.dot(ct_ref[:, pl.ds(k0, tk)], jnp.concatenate(ps, axis=1), preferred_element_type=F32)
        acc_ref[...] = alpha * acc_ref[...] + pv

    n_far = jnp.maximum(i * tq - (tk + MAX_DISTANCE) + tk, 0) // tk

    lta_ref[...] = logits(0)

    def far_pair(p, carry):
        ltb_ref[...] = logits(2 * p + 1)
        attend_tile(2 * p, lta_ref[...], near=False)
        lta_ref[...] = logits(2 * p + 2)
        attend_tile(2 * p + 1, ltb_ref[...], near=False)
        return carry

    lax.fori_loop(0, n_far // 2, far_pair, 0)

    @pl.when(n_far % 2 == 1)
    def _():
        attend_tile(n_far - 1, lta_ref[...], near=False)

    def near_tile(j, carry):
        attend_tile(j, logits(j), near=True)
        return carry

    lax.fori_loop(n_far, nkt, near_tile, 0)

    o_t = acc_ref[...] / l_ref[...]
    for h in range(N_HEADS_A):
        o_h = o_t[:, h * tq:(h + 1) * tq].T.astype(BF16)
        out_ref[:, h * HEAD_DIM:(h + 1) * HEAD_DIM] = jnp.dot(
            o_h, wuv_ref[h], preferred_element_type=F32).astype(out_ref.dtype)


def _t5_bucket_np(n):
    max_exact = N_BUCKETS // 2
    nf = np.maximum(n, 1).astype(np.float32)
    large = max_exact + (np.log(nf / max_exact) / math.log(MAX_DISTANCE / max_exact)
                         * (N_BUCKETS - max_exact)).astype(np.int32)
    large = np.minimum(large, N_BUCKETS - 1)
    return np.where(n < max_exact, n, large)


def _dsa_bias_tiles(t5_table):
    tq, tk = DSA_TQ, DSA_TK
    assert MAX_DISTANCE == 128 and tq == 128 and tk == 256
    sl = np.arange(tk)[:, None]
    tl = np.arange(tq)[None, :]
    idx = np.stack([_t5_bucket_np(np.maximum(tl - sl + off, 0)) for off in (0, 128, 256)])
    far = t5_table[N_BUCKETS - 1].astype(F32)
    near = (t5_table.astype(F32)[idx] - far) * LOG2E
    return jnp.transpose(near, (0, 1, 3, 2)).reshape(3, tk, N_HEADS_A * tq)


def dsa_mixer(aq, pf, c, w_uk, w_uv, t5_table):
    B, L, _ = aq.shape
    tq, tk = DSA_TQ, DSA_TK
    assert L % (DSA_SCORE_TILES * tk) == 0
    nq = L // tq
    topk = min(TOPK_MAX, L // 4)
    assert tk >= topk
    o = KV_RANK
    q_idx = pf[..., o:o + IDX_HEADS * IDX_DIM]
    o += IDX_HEADS * IDX_DIM
    k_idx = pf[..., o:o + IDX_DIM]
    o += IDX_DIM
    w_idx = pf[..., o:o + IDX_HEADS]

    qi = q_idx.reshape(B, nq, tq, IDX_HEADS, IDX_DIM).transpose(0, 1, 3, 2, 4)
    qi = qi.reshape(B, nq, IDX_HEADS * tq, IDX_DIM)
    q_hi = qi.astype(BF16)
    q_lo = (qi - q_hi.astype(F32)).astype(BF16)
    qcat = jnp.concatenate([q_hi, q_hi, q_lo, q_lo], axis=-1)
    k_hi = k_idx.astype(BF16)
    k_lo = (k_idx - k_hi.astype(F32)).astype(BF16)
    khl = jnp.concatenate([k_hi, k_lo], axis=-1)
    wt = (w_idx * (IDX_DIM ** -0.5 * IDX_HEADS ** -0.5)).reshape(B, nq, tq, IDX_HEADS)
    wt = wt.transpose(0, 1, 3, 2)
    ct = jnp.swapaxes(c, 1, 2)
    bias = _dsa_bias_tiles(t5_table)

    hq = N_HEADS_A * tq
    return pl.pallas_call(
        functools.partial(_dsa_kernel, topk=topk),
        grid=(B, nq),
        in_specs=[
            pl.BlockSpec((None, tq, N_HEADS_A * HEAD_DIM), lambda b, i: (b, i, 0)),
            _resident((N_HEADS_A, HEAD_DIM, KV_RANK), lambda b, i: (0, 0, 0)),
            pl.BlockSpec((None, None, IDX_HEADS * tq, 4 * IDX_DIM), lambda b, i: (b, i, 0, 0)),
            pl.BlockSpec((None, None, IDX_HEADS, tq), lambda b, i: (b, i, 0, 0)),
            _resident((None, L, 2 * IDX_DIM), lambda b, i: (b, 0, 0)),
            _resident((None, L, KV_RANK), lambda b, i: (b, 0, 0)),
            _resident((None, KV_RANK, L), lambda b, i: (b, 0, 0)),
            _resident((3, tk, hq), lambda b, i: (0, 0, 0)),
            _resident((N_HEADS_A, KV_RANK, HEAD_DIM), lambda b, i: (0, 0, 0)),
        ],
        out_specs=pl.BlockSpec((None, tq, N_HEADS_A * HEAD_DIM), lambda b, i: (b, i, 0)),
        out_shape=jax.ShapeDtypeStruct((B, L, N_HEADS_A * HEAD_DIM), BF16),
        scratch_shapes=[
            pltpu.VMEM((L, tq), jnp.int32),
            pltpu.VMEM((hq, KV_RANK), BF16),
            pltpu.VMEM((KV_RANK, hq), F32),
            pltpu.VMEM((1, hq), F32),
            pltpu.VMEM((1, hq), F32),
            pltpu.VMEM((tk, hq), F32),
            pltpu.VMEM((tk, hq), F32),
        ],
        compiler_params=_cparams(("parallel", "arbitrary")), name="dsa",
    )(aq, w_uk.astype(BF16), qcat, wt, khl, c, ct, bias, w_uv.astype(BF16))


FOX_AUG = 2 * HEAD_DIM
CK_PARTS = 3


def _fox_kernel(q_ref, k_ref, v_ref, o_ref, m_ref, acc_ref, sa_ref, sb_ref, *, tq, tk):
    i = pl.program_id(2)
    n_full = (i * tq) // tk
    m_ref[...] = jnp.full(m_ref.shape, NEG, F32)
    acc_ref[...] = jnp.zeros(acc_ref.shape, F32)

    def logits(j):
        k0 = pl.multiple_of(j * tk, tk)
        return lax.dot_general(q_ref[...], k_ref[pl.ds(k0, tk), :], (((1,), (1,)), ((), ())),
                               preferred_element_type=F32)

    def step(j, s, masked):
        k0 = pl.multiple_of(j * tk, tk)
        if masked:
            t_pos = i * tq + lax.broadcasted_iota(jnp.int32, (tq, tk), 0)
            s_pos = k0 + lax.broadcasted_iota(jnp.int32, (tq, tk), 1)
            s = jnp.where(s_pos <= t_pos, s, NEG)
        m_old = m_ref[...]
        m_new = jnp.maximum(m_old, s.max(axis=1, keepdims=True))
        p = jnp.exp2(s - m_new).astype(BF16)
        pv = jnp.dot(p, v_ref[pl.ds(k0, tk), :], preferred_element_type=F32)
        acc_ref[...] = jnp.exp2(m_old - m_new) * acc_ref[...] + pv
        m_ref[...] = m_new

    sa_ref[...] = logits(0)

    def full_pair(p, carry):
        sb_ref[...] = logits(2 * p + 1)
        step(2 * p, sa_ref[...], masked=False)
        sa_ref[...] = logits(2 * p + 2)
        step(2 * p + 1, sb_ref[...], masked=False)
        return carry

    lax.fori_loop(0, n_full // 2, full_pair, 0)

    @pl.when(n_full % 2 == 1)
    def _():
        sb_ref[...] = logits(n_full)
        step(n_full - 1, sa_ref[...], masked=False)
        step(n_full, sb_ref[...], masked=True)

    @pl.when(n_full % 2 == 0)
    def _():
        step(n_full, sa_ref[...], masked=True)

    acc = acc_ref[...]
    o_ref[...] = (acc[:, :HEAD_DIM] / acc[:, HEAD_DIM:HEAD_DIM + 1]).astype(o_ref.dtype)


def _split_bf16(x, parts):
    out = []
    for _ in range(parts):
        p = x.astype(BF16)
        out.append(p)
        x = x - p.astype(F32)
    return out


def fox_mixer(q, k, v, log_f, tq=512, tk=1024):
    B, L, _ = q.shape
    tq = min(tq, L)
    tk = min(tk, L)
    assert tk % tq == 0 and L % tk == 0
    H = N_HEADS_B
    heads = lambda a: a.reshape(B, L, H, HEAD_DIM).transpose(0, 2, 1, 3)
    ck = (jnp.cumsum(log_f, axis=1) * LOG2E).transpose(0, 2, 1)
    ck_parts = jnp.stack(_split_bf16(ck, CK_PARTS), axis=-1)
    pad = lambda n: jnp.zeros((B, H, L, n), BF16)
    q_aug = jnp.concatenate([heads(q), jnp.full((B, H, L, CK_PARTS), -1.0, BF16),
                             pad(FOX_AUG - HEAD_DIM - CK_PARTS)], axis=-1)
    k_aug = jnp.concatenate([heads(k), ck_parts, pad(FOX_AUG - HEAD_DIM - CK_PARTS)], axis=-1)
    v_aug = jnp.concatenate([heads(v), jnp.ones((B, H, L, 1), BF16), pad(FOX_AUG - HEAD_DIM - 1)], axis=-1)

    return pl.pallas_call(
        functools.partial(_fox_kernel, tq=tq, tk=tk),
        grid=(B, H, L // tq),
        in_specs=[
            pl.BlockSpec((None, None, tq, FOX_AUG), lambda b, h, i: (b, h, i, 0)),
            pl.BlockSpec((None, None, L, FOX_AUG), lambda b, h, i: (b, h, 0, 0)),
            pl.BlockSpec((None, None, L, FOX_AUG), lambda b, h, i: (b, h, 0, 0)),
        ],
        out_specs=pl.BlockSpec((None, tq, HEAD_DIM), lambda b, h, i: (b, i, h)),
        out_shape=jax.ShapeDtypeStruct((B, L, H * HEAD_DIM), BF16),
        scratch_shapes=[pltpu.VMEM((tq, 1), F32), pltpu.VMEM((tq, FOX_AUG), F32),
                        pltpu.VMEM((tq, tk), F32), pltpu.VMEM((tq, tk), F32)],
        compiler_params=_cparams(("parallel", "parallel", "arbitrary")), name="fox",
    )(q_aug, k_aug, v_aug)


def _out_proj_kernel(x_ref, oa_ref, ob_ref, wa_ref, wb_ref, g_ref, x2_ref, hn_ref):
    y = x_ref[...] + jnp.dot(oa_ref[...], wa_ref[...], preferred_element_type=F32)
    y = y + jnp.dot(ob_ref[...], wb_ref[...], preferred_element_type=F32)
    x2_ref[...] = y
    hn_ref[...] = _rms(y, g_ref[...]).astype(hn_ref.dtype)


def out_proj(x, oa, ob, w_o, g2, tm=512):
    B, L, D = x.shape
    tm = min(tm, L)
    wa = w_o[:N_HEADS_A * HEAD_DIM].astype(BF16)
    wb = w_o[N_HEADS_A * HEAD_DIM:].astype(BF16)
    row = lambda n: pl.BlockSpec((None, tm, n), lambda b, i: (b, i, 0))
    return pl.pallas_call(
        _out_proj_kernel,
        grid=(B, L // tm),
        in_specs=[row(D), row(oa.shape[-1]), row(ob.shape[-1]),
                  _resident(wa.shape, lambda b, i: (0, 0)), _resident(wb.shape, lambda b, i: (0, 0)),
                  pl.BlockSpec((1, D), lambda b, i: (0, 0))],
        out_specs=[row(D), row(D)],
        out_shape=[jax.ShapeDtypeStruct((B, L, D), F32), jax.ShapeDtypeStruct((B, L, D), BF16)],
        compiler_params=_cparams(("parallel", "parallel")), name="out_proj",
    )(x, oa, ob, wa, wb, g2.reshape(1, D))


FFN_HALO = 16


def _ffn_kernel(h_ref, halo_ref, x2_ref, wg_ref, wv_ref, cwg_ref, cwv_ref, cbg_ref, cbv_ref, wd_ref, fg_ref,
                o_ref, hbuf_ref, ug_ref, uv_ref, acc_ref, *, tm):
    i = pl.program_id(1)
    f = pl.program_id(2)

    @pl.when(f == 0)
    def _():
        hbuf_ref[0:FFN_HALO, :] = jnp.where(i > 0, halo_ref[...], jnp.zeros_like(halo_ref))
        hbuf_ref[FFN_HALO:, :] = h_ref[...]
        acc_ref[...] = jnp.zeros(acc_ref.shape, F32)

    ug_ref[...] = jnp.dot(hbuf_ref[...], wg_ref[...], preferred_element_type=F32)
    uv_ref[...] = jnp.dot(hbuf_ref[...], wv_ref[...], preferred_element_type=F32)

    def conv(u_ref, cw_ref, cb_ref):
        a = cb_ref[...] + cw_ref[0:1, :] * u_ref[pl.ds(FFN_HALO - 2, tm), :]
        a = a + cw_ref[1:2, :] * u_ref[pl.ds(FFN_HALO - 1, tm), :]
        return a + cw_ref[2:3, :] * u_ref[pl.ds(FFN_HALO, tm), :]

    gate = conv(ug_ref, cwg_ref, cbg_ref)
    val = conv(uv_ref, cwv_ref, cbv_ref)
    act = (gate * jax.nn.sigmoid(gate) * val).astype(BF16)
    acc_ref[...] += jnp.dot(act, wd_ref[...], preferred_element_type=F32)

    @pl.when(f == pl.num_programs(2) - 1)
    def _():
        o_ref[...] = _rms(x2_ref[...] + acc_ref[...], fg_ref[...]).astype(o_ref.dtype)


def conv_ffn_final(hn, x2, w_up, conv_w, conv_b, w_down, final_g, tm=512, fc=512):
    B, L, D = hn.shape
    F = w_down.shape[0]
    tm = min(tm, L)
    assert F % fc == 0 and tm % FFN_HALO == 0
    nf = F // fc
    wu = w_up.astype(BF16)
    wd = w_down.astype(BF16)
    cb = conv_b.reshape(1, 2 * F)
    hpt = tm // FFN_HALO
    return pl.pallas_call(
        functools.partial(_ffn_kernel, tm=tm),
        grid=(B, L // tm, nf),
        in_specs=[
            pl.BlockSpec((None, tm, D), lambda b, i, f: (b, i, 0)),
            pl.BlockSpec((None, FFN_HALO, D), lambda b, i, f: (b, jnp.maximum(i * hpt - 1, 0), 0)),
            pl.BlockSpec((None, tm, D), lambda b, i, f: (b, i, 0)),
            pl.BlockSpec((D, fc), lambda b, i, f: (0, f)),
            pl.BlockSpec((D, fc), lambda b, i, f: (0, nf + f)),
            pl.BlockSpec((CONV_WIDTH, fc), lambda b, i, f: (0, f)),
            pl.BlockSpec((CONV_WIDTH, fc), lambda b, i, f: (0, nf + f)),
            pl.BlockSpec((1, fc), lambda b, i, f: (0, f)),
            pl.BlockSpec((1, fc), lambda b, i, f: (0, nf + f)),
            pl.BlockSpec((fc, D), lambda b, i, f: (f, 0)),
            pl.BlockSpec((1, D), lambda b, i, f: (0, 0)),
        ],
        out_specs=pl.BlockSpec((None, tm, D), lambda b, i, f: (b, i, 0)),
        out_shape=jax.ShapeDtypeStruct((B, L, D), F32),
        scratch_shapes=[pltpu.VMEM((tm + FFN_HALO, D), BF16),
                        pltpu.VMEM((tm + FFN_HALO, fc), F32),
                        pltpu.VMEM((tm + FFN_HALO, fc), F32),
                        pltpu.VMEM((tm, D), F32)],
        compiler_params=_cparams(("parallel", "parallel", "arbitrary")), name="conv_ffn",
    )(hn, hn, x2, wu, wu, conv_w, conv_w, cb, cb, wd, final_g.reshape(1, D))


def kernel(x, w_in, kv_norm_g, w_uk, w_uv, t5_table, fgate_b, w_o, norm1_g, norm2_g,
           w_up, conv_w, conv_b, w_down, final_g):
    B, L, D = x.shape
    depth = w_in.shape[0]
    na = N_HEADS_A * HEAD_DIM
    nb = N_HEADS_B * HEAD_DIM
    sizes = (na, KV_RANK, IDX_HEADS * IDX_DIM, IDX_DIM, IDX_HEADS, nb, nb, nb, N_HEADS_B)
    offs = np.concatenate([[0], np.cumsum(sizes)])
    n_f32 = KV_RANK + IDX_HEADS * IDX_DIM + IDX_DIM + IDX_HEADS + N_HEADS_B
    n_f32_pad = -(-n_f32 // 128) * 128
    assert depth == 1, "the final-norm FFN epilogue closes a single-layer stack"
    for l in range(depth):
        w = w_in[l]
        w_bq = w[:, offs[5]:offs[6]] * (HEAD_DIM ** -0.5 * LOG2E)
        w_attn = jnp.concatenate([w[:, offs[0]:offs[1]], w_bq, w[:, offs[6]:offs[8]]], axis=1).astype(BF16)
        w_misc = jnp.concatenate([w[:, offs[1]:offs[5]], w[:, offs[8]:offs[9]],
                                  jnp.zeros((D, n_f32_pad - n_f32), w.dtype)], axis=1).astype(BF16)
        pa = norm_proj(x, norm1_g[l], w_attn, BF16, tm=512)
        pf, c = norm_proj(x, norm1_g[l], w_misc, F32, tm=512, kv_g=kv_norm_g[l])
        o_a = dsa_mixer(pa, pf, c, w_uk[l], w_uv[l], t5_table)
        bf = pf[..., n_f32 - N_HEADS_B:n_f32]
        log_f = jax.nn.log_sigmoid(bf + fgate_b[l].astype(F32))
        o_b = fox_mixer(pa[..., na:na + nb], pa[..., na + nb:na + 2 * nb], pa[..., na + 2 * nb:], log_f)
        x2, hn = out_proj(x, o_a, o_b, w_o[l], norm2_g[l])
        return conv_ffn_final(hn, x2, w_up[l], conv_w[l], conv_b[l], w_down[l], final_g)
```

```python
import functools
import math

import numpy as np
import jax
import jax.numpy as jnp
from jax import lax
from jax.experimental import pallas as pl
from jax.experimental.pallas import tpu as pltpu

HEAD_DIM = 128
N_HEADS_A = 8
N_HEADS_B = 8
KV_RANK = 256
IDX_HEADS = 16
IDX_DIM = 64
TOPK_MAX = 256
N_BUCKETS = 32
MAX_DISTANCE = 128
CONV_WIDTH = 3
EPS = 1e-6

F32 = jnp.float32
BF16 = jnp.bfloat16
INT_MIN = -2 ** 31
I16_MIN, I16_MAX = -2 ** 15, 2 ** 15 - 1
NEG = -1e30
LOGIT_SAFE = 60.0
NORM_SLACK = 1.02
UNDERFLOW_BITS = 160.0
LOG2E = math.log2(math.e)

VMEM_LIMIT_BYTES = 56 * 1024 * 1024

DSA_TQ = 128
DSA_TK = 256
IDX_HEAD_GROUP = 4
DSA_SCORE_TILES = 2


def _cparams(sem):
    return pltpu.CompilerParams(dimension_semantics=sem, vmem_limit_bytes=VMEM_LIMIT_BYTES)


def _resident(block_shape, index_map):
    return pl.BlockSpec(block_shape, index_map, pipeline_mode=pl.Buffered(1))


def _rms(x, g):
    ms = jnp.mean(x * x, axis=-1, keepdims=True)
    return x * lax.rsqrt(ms + EPS) * g


def _norm_proj_kernel(x_ref, g_ref, w_ref, o_ref, *, n_chunk):
    h = _rms(x_ref[...], g_ref[...]).astype(BF16)
    n = o_ref.shape[1]
    for c0 in range(0, n, n_chunk):
        c1 = min(c0 + n_chunk, n)
        o_ref[:, c0:c1] = jnp.dot(h, w_ref[:, c0:c1], preferred_element_type=F32).astype(o_ref.dtype)


def _norm_proj_latent_kernel(x_ref, g_ref, w_ref, kvg_ref, o_ref, c_ref, *, n_chunk):
    h = _rms(x_ref[...], g_ref[...]).astype(BF16)
    n = o_ref.shape[1]
    for c0 in range(0, n, n_chunk):
        c1 = min(c0 + n_chunk, n)
        o_ref[:, c0:c1] = jnp.dot(h, w_ref[:, c0:c1], preferred_element_type=F32)
    c_ref[...] = _rms(o_ref[:, 0:KV_RANK], kvg_ref[...]).astype(c_ref.dtype)


def norm_proj(x, g, w, out_dtype, tm, kv_g=None):
    B, L, D = x.shape
    N = w.shape[1]
    tm = min(tm, L)
    grid = (B, L // tm)
    x_spec = pl.BlockSpec((None, tm, D), lambda b, i: (b, i, 0))
    g_spec = pl.BlockSpec((1, D), lambda b, i: (0, 0))
    w_spec = _resident((D, N), lambda b, i: (0, 0))
    o_spec = pl.BlockSpec((None, tm, N), lambda b, i: (b, i, 0))
    if kv_g is None:
        return pl.pallas_call(
            functools.partial(_norm_proj_kernel, n_chunk=512),
            grid=grid, in_specs=[x_spec, g_spec, w_spec], out_specs=o_spec,
            out_shape=jax.ShapeDtypeStruct((B, L, N), out_dtype),
            compiler_params=_cparams(("parallel", "parallel")), name="norm_proj",
        )(x, g.reshape(1, D), w)
    return pl.pallas_call(
        functools.partial(_norm_proj_latent_kernel, n_chunk=512),
        grid=grid,
        in_specs=[x_spec, g_spec, w_spec, pl.BlockSpec((1, KV_RANK), lambda b, i: (0, 0))],
        out_specs=[o_spec, pl.BlockSpec((None, tm, KV_RANK), lambda b, i: (b, i, 0))],
        out_shape=[jax.ShapeDtypeStruct((B, L, N), out_dtype),
                   jax.ShapeDtypeStruct((B, L, KV_RANK), BF16)],
        compiler_params=_cparams(("parallel", "parallel")), name="norm_proj_latent",
    )(x, g.reshape(1, D), w, kv_g.reshape(1, KV_RANK))


def _dsa_kernel(bnd_ref, aq_ref, wuk_ref, qhl_ref, wt_ref, kcat_ref, c_ref, ct_ref, bias_ref, wuv_ref,
                out_ref, s_ref, h_ref, qcat_ref, ql_ref, acc_ref, m_ref, l_ref, lta_ref, ltb_ref, *, topk):
    tq, tk = DSA_TQ, DSA_TK
    i = pl.program_id(1)
    nkt = (i * tq) // tk + 1
    t_pos = i * tq + lax.broadcasted_iota(jnp.int32, (tk, tq), 1)
    s_loc = lax.broadcasted_iota(jnp.int32, (tk, tq), 0)

    scale = HEAD_DIM ** -0.5 * LOG2E
    qn2 = jnp.float32(0.0)
    for h in range(N_HEADS_A):
        ql = jnp.dot(aq_ref[:, h * HEAD_DIM:(h + 1) * HEAD_DIM], wuk_ref[h], preferred_element_type=F32)
        ql = ql * scale
        qn2 = jnp.maximum(qn2, jnp.max(jnp.sum(ql * ql, axis=1, keepdims=True)))
        ql_ref[h * tq:(h + 1) * tq, :] = ql.astype(BF16)

    qcat_ref[...] = jnp.concatenate([qhl_ref[...], qhl_ref[...]], axis=1)

    sub = DSA_SCORE_TILES
    tc = sub * tk
    n_chunks = (i * tq) // tc + 1

    def score_chunk(jc, carry):
        for u in range(sub):
            k0 = pl.multiple_of(jc * tc + u * tk, tk)
            kcat = kcat_ref[pl.ds(k0, tk), :]
            s = jnp.zeros((tk, tq), F32)
            hg = IDX_HEAD_GROUP
            for g in range(IDX_HEADS // hg):
                d = lax.dot_general(kcat, qcat_ref[g * hg * tq:(g + 1) * hg * tq, :],
                                    (((1,), (1,)), ((), ())), preferred_element_type=F32)
                for hh in range(hg):
                    h = g * hg + hh
                    s = s + jnp.maximum(d[:, hh * tq:(hh + 1) * tq], 0.0) * wt_ref[h:h + 1, :]
            s = jnp.where(s == 0.0, 0.0, s)
            bits = lax.bitcast_convert_type(s, jnp.int32)
            key = bits ^ ((bits >> 31) & 0x7FFFFFFF)
            key = jnp.where(k0 + s_loc <= t_pos, key, INT_MIN)
            s_ref[pl.ds(k0, tk), :] = key
            h_ref[pl.ds(k0, tk), :] = (key >> 16).astype(jnp.int16)
        return carry

    lax.fori_loop(0, n_chunks, score_chunk, 0)

    def count_ge(cand):
        def body(jc, acc):
            k0 = pl.multiple_of(jc * tc, tc)
            ge = jnp.where(s_ref[pl.ds(k0, tc), :] >= cand, 1, 0).astype(jnp.int32)
            return acc + ge.reshape(tc // 8, 8, tq).sum(axis=0)
        acc = lax.fori_loop(0, n_chunks, body, jnp.zeros((8, tq), jnp.int32))
        return acc.sum(axis=0, keepdims=True)

    one16, zero16 = jnp.int16(1), jnp.int16(0)

    @pl.when(n_chunks % 2 == 1)
    def _():
        h_ref[pl.ds(pl.multiple_of(n_chunks * tc, tc), tc), :] = jnp.full((tc, tq), I16_MIN, jnp.int16)

    tc2 = 2 * tc
    n_chunks2 = (n_chunks + 1) // 2

    def count_ge16(cand):
        cand16 = cand.astype(jnp.int16)

        def body(jc, acc):
            k0 = pl.multiple_of(jc * tc2, tc2)
            ge = jnp.where(h_ref[pl.ds(k0, tc2), :] >= cand16, one16, zero16)
            parts = [ge[r * 16:(r + 1) * 16, :] for r in range(tc2 // 16)]
            while len(parts) > 1:
                parts = [a + b for a, b in zip(parts[0::2], parts[1::2])]
            return acc + parts[0]
        acc = lax.fori_loop(0, n_chunks2, body, jnp.zeros((16, tq), jnp.int16))
        return acc.astype(jnp.int32).sum(axis=0, keepdims=True)

    def digit_select(need):
        def bit_step(b, d):
            cand = d + lax.shift_left(jnp.int32(1), 15 - b)
            return jnp.where(count_ge16(cand) >= need, cand, d)
        return lax.fori_loop(0, 16, bit_step, jnp.full((1, tq), I16_MIN, jnp.int32))

    d_hi = digit_select(topk)
    above = jnp.where(d_hi == I16_MAX, 0, count_ge16(jnp.minimum(d_hi + 1, I16_MAX)))

    def low_digits(jc, carry):
        k0 = pl.multiple_of(jc * tc, tc)
        key = s_ref[pl.ds(k0, tc), :]
        low = jnp.where((key >> 16) == d_hi, (key & 0xFFFF) + I16_MIN, I16_MIN)
        h_ref[pl.ds(k0, tc), :] = low.astype(jnp.int16)
        return carry

    lax.fori_loop(0, n_chunks, low_digits, 0)
    d_lo = digit_select(topk - above)
    thr = d_hi * 65536 + (d_lo - I16_MIN)
    cnt_thr = count_ge(thr)

    @pl.when(jnp.max(cnt_thr) > topk)
    def _():
        need = (topk - count_ge(thr + 1)).astype(F32)
        tri = jnp.where(lax.broadcasted_iota(jnp.int32, (tk, tk), 0)
                        > lax.broadcasted_iota(jnp.int32, (tk, tk), 1), 1.0, 0.0).astype(BF16)

        def body(j, seen):
            k0 = pl.multiple_of(j * tk, tk)
            x = s_ref[pl.ds(k0, tk), :]
            eq = x == thr
            eqf = jnp.where(eq, 1.0, 0.0)
            rank = jnp.dot(tri, eqf.astype(BF16), preferred_element_type=F32) + seen
            s_ref[pl.ds(k0, tk), :] = jnp.where(eq & (rank >= need), INT_MIN, x)
            return seen + eqf.sum(axis=0, keepdims=True)

        lax.fori_loop(0, nkt, body, jnp.zeros((1, tq), F32))

    def logits(j):
        k0 = pl.multiple_of(j * tk, tk)
        return lax.dot_general(c_ref[pl.ds(k0, tk), :], ql_ref[...],
                               (((1,), (1,)), ((), ())), preferred_element_type=F32)

    def attend_tile(j, lt, near, bounded):
        k0 = pl.multiple_of(j * tk, tk)
        sel = s_ref[pl.ds(k0, tk), :] >= thr
        if near:
            sel = sel & (k0 + s_loc <= t_pos)
            variant = jnp.minimum((i * tq - k0) // MAX_DISTANCE, 2)
            lt = lt + bias_ref[variant]
        ps, ms, sums = [], [], []
        if not bounded:
            m_old = m_ref[...]
        for h in range(N_HEADS_A):
            z = jnp.where(sel, lt[:, h * tq:(h + 1) * tq], NEG)
            if not bounded:
                m_new = jnp.maximum(m_old[:, h * tq:(h + 1) * tq], z.max(axis=0, keepdims=True))
                ms.append(m_new)
                z = z - m_new
            p = jnp.exp2(z)
            sums.append(p.sum(axis=0, keepdims=True))
            ps.append(p.astype(BF16))
        pv = jnp.dot(ct_ref[:, pl.ds(k0, tk)], jnp.concatenate(ps, axis=1), preferred_element_type=F32)
        if bounded:
            l_ref[...] += jnp.concatenate(sums, axis=1)
            acc_ref[...] += pv
        else:
            m_new = jnp.concatenate(ms, axis=1)
            alpha = jnp.exp2(m_old - m_new)
            l_ref[...] = alpha * l_ref[...] + jnp.concatenate(sums, axis=1)
            m_ref[...] = m_new
            acc_ref[...] = alpha * acc_ref[...] + pv

    n_far = jnp.maximum(i * tq - (tk + MAX_DISTANCE) + tk, 0) // tk

    def attend_all(bounded):
        m_ref[...] = jnp.full(m_ref.shape, NEG, F32)
        l_ref[...] = jnp.zeros(l_ref.shape, F32)
        acc_ref[...] = jnp.zeros(acc_ref.shape, F32)
        lta_ref[...] = logits(0)

        def far_pair(p, carry):
            ltb_ref[...] = logits(2 * p + 1)
            attend_tile(2 * p, lta_ref[...], False, bounded)
            lta_ref[...] = logits(2 * p + 2)
            attend_tile(2 * p + 1, ltb_ref[...], False, bounded)
            return carry

        lax.fori_loop(0, n_far // 2, far_pair, 0)

        @pl.when(n_far % 2 == 1)
        def _():
            attend_tile(n_far - 1, lta_ref[...], False, bounded)

        def near_tile(j, carry):
            attend_tile(j, logits(j), True, bounded)
            return carry

        lax.fori_loop(n_far, nkt, near_tile, 0)

    logit_bound = jnp.sqrt(qn2) * bnd_ref[pl.program_id(0), 0] * NORM_SLACK + bnd_ref[pl.program_id(0), 1]
    safe = logit_bound <= LOGIT_SAFE

    @pl.when(safe)
    def _():
        attend_all(bounded=True)

    @pl.when(jnp.logical_not(safe))
    def _():
        attend_all(bounded=False)

    o_t = acc_ref[...] / l_ref[...]
    for h in range(N_HEADS_A):
        o_h = o_t[:, h * tq:(h + 1) * tq].T.astype(BF16)
        out_ref[:, h * HEAD_DIM:(h + 1) * HEAD_DIM] = jnp.dot(
            o_h, wuv_ref[h], preferred_element_type=F32).astype(out_ref.dtype)


def _t5_bucket_np(n):
    max_exact = N_BUCKETS // 2
    nf = np.maximum(n, 1).astype(np.float32)
    large = max_exact + (np.log(nf / max_exact) / math.log(MAX_DISTANCE / max_exact)
                         * (N_BUCKETS - max_exact)).astype(np.int32)
    large = np.minimum(large, N_BUCKETS - 1)
    return np.where(n < max_exact, n, large)


def _dsa_bias_tiles(t5_table):
    tq, tk = DSA_TQ, DSA_TK
    assert MAX_DISTANCE == 128 and tq == 128 and tk == 256
    sl = np.arange(tk)[:, None]
    tl = np.arange(tq)[None, :]
    idx = np.stack([_t5_bucket_np(np.maximum(tl - sl + off, 0)) for off in (0, 128, 256)])
    table = (t5_table.astype(F32) - t5_table[N_BUCKETS - 1].astype(F32)) * LOG2E
    onehot = (jnp.asarray(idx)[..., None] == jnp.arange(N_BUCKETS)).astype(F32)
    near = jnp.einsum('vstn,nh->vsht', onehot, table, precision=lax.Precision.HIGHEST)
    return near.reshape(3, tk, N_HEADS_A * tq), jnp.max(jnp.abs(table))


def dsa_mixer(aq, pf, c, w_uk, w_uv, t5_table):
    B, L, _ = aq.shape
    tq, tk = DSA_TQ, DSA_TK
    assert L % (2 * DSA_SCORE_TILES * tk) == 0
    nq = L // tq
    topk = min(TOPK_MAX, L // 4)
    assert tk >= topk
    o = KV_RANK
    q_idx = pf[..., o:o + IDX_HEADS * IDX_DIM]
    o += IDX_HEADS * IDX_DIM
    k_idx = pf[..., o:o + IDX_DIM]
    o += IDX_DIM
    w_idx = pf[..., o:o + IDX_HEADS]

    qi = q_idx.reshape(B, nq, tq, IDX_HEADS, IDX_DIM).transpose(0, 1, 3, 2, 4)
    qi = qi.reshape(B, nq, IDX_HEADS * tq, IDX_DIM)
    q_hi = qi.astype(BF16)
    q_lo = (qi - q_hi.astype(F32)).astype(BF16)
    qhl = jnp.concatenate([q_hi, q_lo], axis=-1)
    k_hi = k_idx.astype(BF16)
    k_lo = (k_idx - k_hi.astype(F32)).astype(BF16)
    kcat = jnp.concatenate([k_hi, k_hi, k_lo, k_lo], axis=-1)
    wt = (w_idx * (IDX_DIM ** -0.5 * IDX_HEADS ** -0.5)).reshape(B, nq, tq, IDX_HEADS)
    wt = wt.transpose(0, 1, 3, 2)
    ct = jnp.swapaxes(c, 1, 2)
    bias, bias_max = _dsa_bias_tiles(t5_table)
    c_norm = jnp.sqrt(jnp.max(jnp.sum(jnp.square(c.astype(F32)), axis=-1), axis=-1))
    bounds = jnp.stack([c_norm, jnp.broadcast_to(bias_max, (B,))], axis=-1)

    hq = N_HEADS_A * tq
    return pl.pallas_call(
        functools.partial(_dsa_kernel, topk=topk),
        grid=(B, nq),
        in_specs=[
            pl.BlockSpec(memory_space=pltpu.SMEM),
            pl.BlockSpec((None, tq, N_HEADS_A * HEAD_DIM), lambda b, i: (b, i, 0)),
            _resident((N_HEADS_A, HEAD_DIM, KV_RANK), lambda b, i: (0, 0, 0)),
            pl.BlockSpec((None, None, IDX_HEADS * tq, 2 * IDX_DIM), lambda b, i: (b, i, 0, 0)),
            pl.BlockSpec((None, None, IDX_HEADS, tq), lambda b, i: (b, i, 0, 0)),
            _resident((None, L, 4 * IDX_DIM), lambda b, i: (b, 0, 0)),
            _resident((None, L, KV_RANK), lambda b, i: (b, 0, 0)),
            _resident((None, KV_RANK, L), lambda b, i: (b, 0, 0)),
            _resident((3, tk, hq), lambda b, i: (0, 0, 0)),
            _resident((N_HEADS_A, KV_RANK, HEAD_DIM), lambda b, i: (0, 0, 0)),
        ],
        out_specs=pl.BlockSpec((None, tq, N_HEADS_A * HEAD_DIM), lambda b, i: (b, i, 0)),
        out_shape=jax.ShapeDtypeStruct((B, L, N_HEADS_A * HEAD_DIM), BF16),
        scratch_shapes=[
            pltpu.VMEM((L, tq), jnp.int32),
            pltpu.VMEM((L, tq), jnp.int16),
            pltpu.VMEM((IDX_HEADS * tq, 4 * IDX_DIM), BF16),
            pltpu.VMEM((hq, KV_RANK), BF16),
            pltpu.VMEM((KV_RANK, hq), F32),
            pltpu.VMEM((1, hq), F32),
            pltpu.VMEM((1, hq), F32),
            pltpu.VMEM((tk, hq), F32),
            pltpu.VMEM((tk, hq), F32),
        ],
        compiler_params=_cparams(("parallel", "arbitrary")), name="dsa",
    )(bounds, aq, w_uk.astype(BF16), qhl, wt, kcat, c, ct, bias, w_uv.astype(BF16))


FOX_AUG = 2 * HEAD_DIM
CK_PARTS = 3


def _fox_kernel(st_ref, q_ref, k_ref, v_ref, o_ref, m_ref, acc_ref, sa_ref, sb_ref, *, tq, tk):
    i = pl.program_id(2)
    n_full = (i * tq) // tk
    m_ref[...] = jnp.full(m_ref.shape, NEG, F32)
    acc_ref[...] = jnp.zeros(acc_ref.shape, F32)

    def logits(j):
        k0 = pl.multiple_of(j * tk, tk)
        return lax.dot_general(q_ref[...], k_ref[pl.ds(k0, tk), :], (((1,), (1,)), ((), ())),
                               preferred_element_type=F32)

    def step(j, s, masked):
        k0 = pl.multiple_of(j * tk, tk)
        if masked:
            t_pos = i * tq + lax.broadcasted_iota(jnp.int32, (tq, tk), 0)
            s_pos = k0 + lax.broadcasted_iota(jnp.int32, (tq, tk), 1)
            s = jnp.where(s_pos <= t_pos, s, NEG)
        m_old = m_ref[...]
        m_new = jnp.maximum(m_old, s.max(axis=1, keepdims=True))
        p = jnp.exp2(s - m_new).astype(BF16)
        pv = jnp.dot(p, v_ref[pl.ds(k0, tk), :], preferred_element_type=F32)
        acc_ref[...] = jnp.exp2(m_old - m_new) * acc_ref[...] + pv
        m_ref[...] = m_new

    b, h = pl.program_id(0), pl.program_id(1)
    nq, nk = pl.num_programs(2), k_ref.shape[0] // tk
    q_norm = st_ref[b, h, i]
    gate_q0 = st_ref[b, h, nq + i]
    reach = q_norm * NORM_SLACK

    def skip_scan(j, first):
        k_norm = jnp.maximum(st_ref[b, h, 2 * nq + j], st_ref[b, h, 2 * nq + n_full])
        gap = gate_q0 - st_ref[b, h, 2 * nq + nk + j]
        dead = reach * 2.0 * k_norm + gap < -UNDERFLOW_BITS
        return jnp.where(jnp.logical_and(dead, first == j), j + 1, first)

    first = lax.fori_loop(0, n_full, skip_scan, jnp.int32(0))
    n_live = n_full - first

    sa_ref[...] = logits(first)

    def full_pair(p, carry):
        j = first + 2 * p
        sb_ref[...] = logits(j + 1)
        step(j, sa_ref[...], masked=False)
        sa_ref[...] = logits(j + 2)
        step(j + 1, sb_ref[...], masked=False)
        return carry

    lax.fori_loop(0, n_live // 2, full_pair, 0)

    @pl.when(n_live % 2 == 1)
    def _():
        sb_ref[...] = logits(n_full)
        step(n_full - 1, sa_ref[...], masked=False)
        step(n_full, sb_ref[...], masked=True)

    @pl.when(n_live % 2 == 0)
    def _():
        step(n_full, sa_ref[...], masked=True)

    acc = acc_ref[...]
    o_ref[...] = (acc[:, :HEAD_DIM] / acc[:, HEAD_DIM:HEAD_DIM + 1]).astype(o_ref.dtype)


def _split_bf16(x, parts):
    out = []
    for _ in range(parts):
        p = x.astype(BF16)
        out.append(p)
        x = x - p.astype(F32)
    return out


def fox_mixer(q, k, v, log_f, tq=512, tk=1024):
    B, L, _ = q.shape
    tq = min(tq, L)
    tk = min(tk, L)
    assert tk % tq == 0 and L % tk == 0
    H = N_HEADS_B
    heads = lambda a: a.reshape(B, L, H, HEAD_DIM).transpose(0, 2, 1, 3)
    ck = (jnp.cumsum(log_f, axis=1) * LOG2E).transpose(0, 2, 1)
    ck_parts = jnp.stack(_split_bf16(ck, CK_PARTS), axis=-1)
    pad = lambda n: jnp.zeros((B, H, L, n), BF16)
    q_aug = jnp.concatenate([heads(q), jnp.full((B, H, L, CK_PARTS), -1.0, BF16),
                             pad(FOX_AUG - HEAD_DIM - CK_PARTS)], axis=-1)
    k_aug = jnp.concatenate([heads(k), ck_parts, pad(FOX_AUG - HEAD_DIM - CK_PARTS)], axis=-1)
    v_aug = jnp.concatenate([heads(v), jnp.ones((B, H, L, 1), BF16), pad(FOX_AUG - HEAD_DIM - 1)], axis=-1)

    norm = lambda a: jnp.sqrt(jnp.sum(jnp.square(heads(a).astype(F32)), axis=-1))
    stats = jnp.concatenate([
        norm(q).reshape(B, H, L // tq, tq).max(axis=-1),
        ck[:, :, ::tq],
        norm(k).reshape(B, H, L // tk, tk).max(axis=-1),
        ck[:, :, tk - 1::tk],
    ], axis=-1)

    return pl.pallas_call(
        functools.partial(_fox_kernel, tq=tq, tk=tk),
        grid=(B, H, L // tq),
        in_specs=[
            pl.BlockSpec(memory_space=pltpu.SMEM),
            pl.BlockSpec((None, None, tq, FOX_AUG), lambda b, h, i: (b, h, i, 0)),
            pl.BlockSpec((None, None, L, FOX_AUG), lambda b, h, i: (b, h, 0, 0)),
            pl.BlockSpec((None, None, L, FOX_AUG), lambda b, h, i: (b, h, 0, 0)),
        ],
        out_specs=pl.BlockSpec((None, tq, HEAD_DIM), lambda b, h, i: (b, i, h)),
        out_shape=jax.ShapeDtypeStruct((B, L, H * HEAD_DIM), BF16),
        scratch_shapes=[pltpu.VMEM((tq, 1), F32), pltpu.VMEM((tq, FOX_AUG), F32),
                        pltpu.VMEM((tq, tk), F32), pltpu.VMEM((tq, tk), F32)],
        compiler_params=_cparams(("parallel", "parallel", "arbitrary")), name="fox",
    )(stats, q_aug, k_aug, v_aug)


def _out_proj_kernel(x_ref, oa_ref, ob_ref, wa_ref, wb_ref, g_ref, x2_ref, hn_ref):
    y = x_ref[...] + jnp.dot(oa_ref[...], wa_ref[...], preferred_element_type=F32)
    y = y + jnp.dot(ob_ref[...], wb_ref[...], preferred_element_type=F32)
    x2_ref[...] = y
    hn_ref[...] = _rms(y, g_ref[...]).astype(hn_ref.dtype)


def out_proj(x, oa, ob, w_o, g2, tm=512):
    B, L, D = x.shape
    tm = min(tm, L)
    wa = w_o[:N_HEADS_A * HEAD_DIM].astype(BF16)
    wb = w_o[N_HEADS_A * HEAD_DIM:].astype(BF16)
    row = lambda n: pl.BlockSpec((None, tm, n), lambda b, i: (b, i, 0))
    return pl.pallas_call(
        _out_proj_kernel,
        grid=(B, L // tm),
        in_specs=[row(D), row(oa.shape[-1]), row(ob.shape[-1]),
                  _resident(wa.shape, lambda b, i: (0, 0)), _resident(wb.shape, lambda b, i: (0, 0)),
                  pl.BlockSpec((1, D), lambda b, i: (0, 0))],
        out_specs=[row(D), row(D)],
        out_shape=[jax.ShapeDtypeStruct((B, L, D), F32), jax.ShapeDtypeStruct((B, L, D), BF16)],
        compiler_params=_cparams(("parallel", "parallel")), name="out_proj",
    )(x, oa, ob, wa, wb, g2.reshape(1, D))


FFN_HALO = 16


def _ffn_kernel(h_ref, halo_ref, x2_ref, wg_ref, wv_ref, cwg_ref, cwv_ref, cbg_ref, cbv_ref, wd_ref, fg_ref,
                o_ref, hbuf_ref, ug_ref, uv_ref, acc_ref, *, tm):
    i = pl.program_id(1)
    f = pl.program_id(2)

    @pl.when(f == 0)
    def _():
        hbuf_ref[0:FFN_HALO, :] = jnp.where(i > 0, halo_ref[...], jnp.zeros_like(halo_ref))
        hbuf_ref[FFN_HALO:, :] = h_ref[...]
        acc_ref[...] = jnp.zeros(acc_ref.shape, F32)

    ug_ref[...] = jnp.dot(hbuf_ref[...], wg_ref[...], preferred_element_type=F32)
    uv_ref[...] = jnp.dot(hbuf_ref[...], wv_ref[...], preferred_element_type=F32)

    def conv(u_ref, cw_ref, cb_ref):
        a = cb_ref[...] + cw_ref[0:1, :] * u_ref[pl.ds(FFN_HALO - 2, tm), :]
        a = a + cw_ref[1:2, :] * u_ref[pl.ds(FFN_HALO - 1, tm), :]
        return a + cw_ref[2:3, :] * u_ref[pl.ds(FFN_HALO, tm), :]

    gate = conv(ug_ref, cwg_ref, cbg_ref)
    val = conv(uv_ref, cwv_ref, cbv_ref)
    act = (gate * jax.nn.sigmoid(gate) * val).astype(BF16)
    acc_ref[...] += jnp.dot(act, wd_ref[...], preferred_element_type=F32)

    @pl.when(f == pl.num_programs(2) - 1)
    def _():
        o_ref[...] = _rms(x2_ref[...] + acc_ref[...], fg_ref[...]).astype(o_ref.dtype)


def conv_ffn_final(hn, x2, w_up, conv_w, conv_b, w_down, final_g, tm=512, fc=512):
    B, L, D = hn.shape
    F = w_down.shape[0]
    tm = min(tm, L)
    assert F % fc == 0 and tm % FFN_HALO == 0
    nf = F // fc
    wu = w_up.astype(BF16)
    wd = w_down.astype(BF16)
    cb = conv_b.reshape(1, 2 * F)
    hpt = tm // FFN_HALO
    return pl.pallas_call(
        functools.partial(_ffn_kernel, tm=tm),
        grid=(B, L // tm, nf),
        in_specs=[
            pl.BlockSpec((None, tm, D), lambda b, i, f: (b, i, 0)),
            pl.BlockSpec((None, FFN_HALO, D), lambda b, i, f: (b, jnp.maximum(i * hpt - 1, 0), 0)),
            pl.BlockSpec((None, tm, D), lambda b, i, f: (b, i, 0)),
            pl.BlockSpec((D, fc), lambda b, i, f: (0, f)),
            pl.BlockSpec((D, fc), lambda b, i, f: (0, nf + f)),
            pl.BlockSpec((CONV_WIDTH, fc), lambda b, i, f: (0, f)),
            pl.BlockSpec((CONV_WIDTH, fc), lambda b, i, f: (0, nf + f)),
            pl.BlockSpec((1, fc), lambda b, i, f: (0, f)),
            pl.BlockSpec((1, fc), lambda b, i, f: (0, nf + f)),
            pl.BlockSpec((fc, D), lambda b, i, f: (f, 0)),
            pl.BlockSpec((1, D), lambda b, i, f: (0, 0)),
        ],
        out_specs=pl.BlockSpec((None, tm, D), lambda b, i, f: (b, i, 0)),
        out_shape=jax.ShapeDtypeStruct((B, L, D), F32),
        scratch_shapes=[pltpu.VMEM((tm + FFN_HALO, D), BF16),
                        pltpu.VMEM((tm + FFN_HALO, fc), F32),
                        pltpu.VMEM((tm + FFN_HALO, fc), F32),
                        pltpu.VMEM((tm, D), F32)],
        compiler_params=_cparams(("parallel", "parallel", "arbitrary")), name="conv_ffn",
    )(hn, hn, x2, wu, wu, conv_w, conv_w, cb, cb, wd, final_g.reshape(1, D))


def kernel(x, w_in, kv_norm_g, w_uk, w_uv, t5_table, fgate_b, w_o, norm1_g, norm2_g,
           w_up, conv_w, conv_b, w_down, final_g):
    B, L, D = x.shape
    depth = w_in.shape[0]
    na = N_HEADS_A * HEAD_DIM
    nb = N_HEADS_B * HEAD_DIM
    sizes = (na, KV_RANK, IDX_HEADS * IDX_DIM, IDX_DIM, IDX_HEADS, nb, nb, nb, N_HEADS_B)
    offs = np.concatenate([[0], np.cumsum(sizes)])
    n_f32 = KV_RANK + IDX_HEADS * IDX_DIM + IDX_DIM + IDX_HEADS + N_HEADS_B
    n_f32_pad = -(-n_f32 // 128) * 128
    assert depth == 1, "the final-norm FFN epilogue closes a single-layer stack"
    for l in range(depth):
        w = w_in[l]
        w_bq = w[:, offs[5]:offs[6]] * (HEAD_DIM ** -0.5 * LOG2E)
        w_attn = jnp.concatenate([w[:, offs[0]:offs[1]], w_bq, w[:, offs[6]:offs[8]]], axis=1).astype(BF16)
        w_misc = jnp.concatenate([w[:, offs[1]:offs[5]], w[:, offs[8]:offs[9]],
                                  jnp.zeros((D, n_f32_pad - n_f32), w.dtype)], axis=1).astype(BF16)
        pa = norm_proj(x, norm1_g[l], w_attn, BF16, tm=512)
        pf, c = norm_proj(x, norm1_g[l], w_misc, F32, tm=512, kv_g=kv_norm_g[l])
        o_a = dsa_mixer(pa, pf, c, w_uk[l], w_uv[l], t5_table)
        bf = pf[..., n_f32 - N_HEADS_B:n_f32]
        log_f = jax.nn.log_sigmoid(bf + fgate_b[l].astype(F32))
        o_b = fox_mixer(pa[..., na:na + nb], pa[..., na + nb:na + 2 * nb], pa[..., na + 2 * nb:], log_f)
        x2, hn = out_proj(x, o_a, o_b, w_o[l], norm2_g[l])
        return conv_ffn_final(hn, x2, w_up[l], conv_w[l], conv_b[l], w_down[l], final_g)
```

```python
import functools
import math

import numpy as np
import jax
import jax.numpy as jnp
from jax import lax
from jax.experimental import pallas as pl
from jax.experimental.pallas import tpu as pltpu

HEAD_DIM = 128
N_HEADS_A = 8
N_HEADS_B = 8
KV_RANK = 256
IDX_HEADS = 16
IDX_DIM = 64
TOPK_MAX = 256
N_BUCKETS = 32
MAX_DISTANCE = 128
CONV_WIDTH = 3
EPS = 1e-6

F32 = jnp.float32
BF16 = jnp.bfloat16
INT_MIN = -2 ** 31
I16_MIN, I16_MAX = -2 ** 15, 2 ** 15 - 1
NEG = -1e30
LOGIT_SAFE = 60.0
NORM_SLACK = 1.02
UNDERFLOW_BITS = 160.0
LOG2E = math.log2(math.e)

VMEM_LIMIT_BYTES = 56 * 1024 * 1024

DSA_TQ = 128
DSA_TK = 256
IDX_HEAD_GROUP = 4
DSA_SCORE_TILES = 2


def _cparams(sem):
    return pltpu.CompilerParams(dimension_semantics=sem, vmem_limit_bytes=VMEM_LIMIT_BYTES)


def _resident(block_shape, index_map):
    return pl.BlockSpec(block_shape, index_map, pipeline_mode=pl.Buffered(1))


def _rms(x, g):
    ms = jnp.mean(x * x, axis=-1, keepdims=True)
    return x * lax.rsqrt(ms + EPS) * g


def _norm_proj_kernel(x_ref, g_ref, w_ref, o_ref, *, n_chunk):
    h = _rms(x_ref[...], g_ref[...]).astype(BF16)
    n = o_ref.shape[1]
    for c0 in range(0, n, n_chunk):
        c1 = min(c0 + n_chunk, n)
        o_ref[:, c0:c1] = jnp.dot(h, w_ref[:, c0:c1], preferred_element_type=F32).astype(o_ref.dtype)


def _norm_proj_latent_kernel(x_ref, g_ref, w_ref, kvg_ref, o_ref, c_ref, *, n_chunk):
    h = _rms(x_ref[...], g_ref[...]).astype(BF16)
    n = o_ref.shape[1]
    for c0 in range(0, n, n_chunk):
        c1 = min(c0 + n_chunk, n)
        o_ref[:, c0:c1] = jnp.dot(h, w_ref[:, c0:c1], preferred_element_type=F32)
    c_ref[...] = _rms(o_ref[:, 0:KV_RANK], kvg_ref[...]).astype(c_ref.dtype)


def norm_proj(x, g, w, out_dtype, tm, kv_g=None):
    B, L, D = x.shape
    N = w.shape[1]
    tm = min(tm, L)
    grid = (B, L // tm)
    x_spec = pl.BlockSpec((None, tm, D), lambda b, i: (b, i, 0))
    g_spec = pl.BlockSpec((1, D), lambda b, i: (0, 0))
    w_spec = _resident((D, N), lambda b, i: (0, 0))
    o_spec = pl.BlockSpec((None, tm, N), lambda b, i: (b, i, 0))
    if kv_g is None:
        return pl.pallas_call(
            functools.partial(_norm_proj_kernel, n_chunk=512),
            grid=grid, in_specs=[x_spec, g_spec, w_spec], out_specs=o_spec,
            out_shape=jax.ShapeDtypeStruct((B, L, N), out_dtype),
            compiler_params=_cparams(("parallel", "parallel")), name="norm_proj",
        )(x, g.reshape(1, D), w)
    return pl.pallas_call(
        functools.partial(_norm_proj_latent_kernel, n_chunk=512),
        grid=grid,
        in_specs=[x_spec, g_spec, w_spec, pl.BlockSpec((1, KV_RANK), lambda b, i: (0, 0))],
        out_specs=[o_spec, pl.BlockSpec((None, tm, KV_RANK), lambda b, i: (b, i, 0))],
        out_shape=[jax.ShapeDtypeStruct((B, L, N), out_dtype),
                   jax.ShapeDtypeStruct((B, L, KV_RANK), BF16)],
        compiler_params=_cparams(("parallel", "parallel")), name="norm_proj_latent",
    )(x, g.reshape(1, D), w, kv_g.reshape(1, KV_RANK))


def _dsa_kernel(bnd_ref, aq_ref, wuk_ref, qhl_ref, wt_ref, kcat_ref, c_ref, ct_ref, bias_ref, wuv_ref,
                out_ref, s_ref, h_ref, qcat_ref, ql_ref, acc_ref, m_ref, l_ref, lta_ref, ltb_ref, *, topk,
                digit16):
    tq, tk = DSA_TQ, DSA_TK
    i = pl.program_id(1)
    nkt = (i * tq) // tk + 1
    t_pos = i * tq + lax.broadcasted_iota(jnp.int32, (tk, tq), 1)
    s_loc = lax.broadcasted_iota(jnp.int32, (tk, tq), 0)

    scale = HEAD_DIM ** -0.5 * LOG2E
    qn2 = jnp.float32(0.0)
    for h in range(N_HEADS_A):
        ql = jnp.dot(aq_ref[:, h * HEAD_DIM:(h + 1) * HEAD_DIM], wuk_ref[h], preferred_element_type=F32)
        ql = ql * scale
        qn2 = jnp.maximum(qn2, jnp.max(jnp.sum(ql * ql, axis=1, keepdims=True)))
        ql_ref[h * tq:(h + 1) * tq, :] = ql.astype(BF16)

    qcat_ref[...] = jnp.concatenate([qhl_ref[...], qhl_ref[...]], axis=1)

    sub = DSA_SCORE_TILES
    tc = sub * tk
    n_chunks = (i * tq) // tc + 1

    def score_chunk(jc, carry):
        for u in range(sub):
            k0 = pl.multiple_of(jc * tc + u * tk, tk)
            kcat = kcat_ref[pl.ds(k0, tk), :]
            s = jnp.zeros((tk, tq), F32)
            hg = IDX_HEAD_GROUP
            for g in range(IDX_HEADS // hg):
                d = lax.dot_general(kcat, qcat_ref[g * hg * tq:(g + 1) * hg * tq, :],
                                    (((1,), (1,)), ((), ())), preferred_element_type=F32)
                for hh in range(hg):
                    h = g * hg + hh
                    s = s + jnp.maximum(d[:, hh * tq:(hh + 1) * tq], 0.0) * wt_ref[h:h + 1, :]
            s = jnp.where(s == 0.0, 0.0, s)
            bits = lax.bitcast_convert_type(s, jnp.int32)
            key = bits ^ ((bits >> 31) & 0x7FFFFFFF)
            key = jnp.where(k0 + s_loc <= t_pos, key, INT_MIN)
            s_ref[pl.ds(k0, tk), :] = key
            if digit16:
                h_ref[pl.ds(k0, tk), :] = (key >> 16).astype(jnp.int16)
        return carry

    lax.fori_loop(0, n_chunks, score_chunk, 0)

    def count_ge(cand):
        def body(jc, acc):
            k0 = pl.multiple_of(jc * tc, tc)
            ge = jnp.where(s_ref[pl.ds(k0, tc), :] >= cand, 1, 0).astype(jnp.int32)
            return acc + ge.reshape(tc // 8, 8, tq).sum(axis=0)
        acc = lax.fori_loop(0, n_chunks, body, jnp.zeros((8, tq), jnp.int32))
        return acc.sum(axis=0, keepdims=True)

    one16, zero16 = jnp.int16(1), jnp.int16(0)

    @pl.when(n_chunks % 2 == 1)
    def _():
        h_ref[pl.ds(pl.multiple_of(n_chunks * tc, tc), tc), :] = jnp.full((tc, tq), I16_MIN, jnp.int16)

    tc2 = 2 * tc
    n_chunks2 = (n_chunks + 1) // 2

    def count_ge16(cand):
        cand16 = cand.astype(jnp.int16)

        def body(jc, acc):
            k0 = pl.multiple_of(jc * tc2, tc2)
            ge = jnp.where(h_ref[pl.ds(k0, tc2), :] >= cand16, one16, zero16)
            parts = [ge[r * 16:(r + 1) * 16, :] for r in range(tc2 // 16)]
            while len(parts) > 1:
                parts = [a + b for a, b in zip(parts[0::2], parts[1::2])]
            return acc + parts[0]
        acc = lax.fori_loop(0, n_chunks2, body, jnp.zeros((16, tq), jnp.int16))
        return acc.astype(jnp.int32).sum(axis=0, keepdims=True)

    def digit_select(need):
        def bit_step(b, d):
            cand = d + lax.shift_left(jnp.int32(1), 15 - b)
            return jnp.where(count_ge16(cand) >= need, cand, d)
        return lax.fori_loop(0, 16, bit_step, jnp.full((1, tq), I16_MIN, jnp.int32))

    if digit16:
        d_hi = digit_select(topk)
        above = jnp.where(d_hi == I16_MAX, 0, count_ge16(jnp.minimum(d_hi + 1, I16_MAX)))

        def low_digits(jc, carry):
            k0 = pl.multiple_of(jc * tc, tc)
            key = s_ref[pl.ds(k0, tc), :]
            low = jnp.where((key >> 16) == d_hi, (key & 0xFFFF) + I16_MIN, I16_MIN)
            h_ref[pl.ds(k0, tc), :] = low.astype(jnp.int16)
            return carry

        lax.fori_loop(0, n_chunks, low_digits, 0)
        d_lo = digit_select(topk - above)
        thr = d_hi * 65536 + (d_lo - I16_MIN)
        cnt_thr = count_ge(thr)
    else:
        def bit_step32(b, carry):
            t, cnt_t = carry
            cand = t + lax.shift_left(jnp.int32(1), 31 - b)
            cnt = count_ge(cand)
            take = cnt >= topk
            return jnp.where(take, cand, t), jnp.where(take, cnt, cnt_t)

        thr, cnt_thr = lax.fori_loop(0, 32, bit_step32, (jnp.full((1, tq), INT_MIN, jnp.int32),
                                                         jnp.zeros((1, tq), jnp.int32) + n_chunks * tc))

    @pl.when(jnp.max(cnt_thr) > topk)
    def _():
        need = (topk - count_ge(thr + 1)).astype(F32)
        tri = jnp.where(lax.broadcasted_iota(jnp.int32, (tk, tk), 0)
                        > lax.broadcasted_iota(jnp.int32, (tk, tk), 1), 1.0, 0.0).astype(BF16)

        def body(j, seen):
            k0 = pl.multiple_of(j * tk, tk)
            x = s_ref[pl.ds(k0, tk), :]
            eq = x == thr
            eqf = jnp.where(eq, 1.0, 0.0)
            rank = jnp.dot(tri, eqf.astype(BF16), preferred_element_type=F32) + seen
            s_ref[pl.ds(k0, tk), :] = jnp.where(eq & (rank >= need), INT_MIN, x)
            return seen + eqf.sum(axis=0, keepdims=True)

        lax.fori_loop(0, nkt, body, jnp.zeros((1, tq), F32))

    def logits(j):
        k0 = pl.multiple_of(j * tk, tk)
        return lax.dot_general(c_ref[pl.ds(k0, tk), :], ql_ref[...],
                               (((1,), (1,)), ((), ())), preferred_element_type=F32)

    def attend_tile(j, lt, near, bounded):
        k0 = pl.multiple_of(j * tk, tk)
        sel = s_ref[pl.ds(k0, tk), :] >= thr
        if near:
            sel = sel & (k0 + s_loc <= t_pos)
            variant = jnp.minimum((i * tq - k0) // MAX_DISTANCE, 2)
            lt = lt + bias_ref[variant]
        ps, ms, sums = [], [], []
        if not bounded:
            m_old = m_ref[...]
        for h in range(N_HEADS_A):
            z = jnp.where(sel, lt[:, h * tq:(h + 1) * tq], NEG)
            if not bounded:
                m_new = jnp.maximum(m_old[:, h * tq:(h + 1) * tq], z.max(axis=0, keepdims=True))
                ms.append(m_new)
                z = z - m_new
            p = jnp.exp2(z)
            sums.append(p.sum(axis=0, keepdims=True))
            ps.append(p.astype(BF16))
        pv = jnp.dot(ct_ref[:, pl.ds(k0, tk)], jnp.concatenate(ps, axis=1), preferred_element_type=F32)
        if bounded:
            l_ref[...] += jnp.concatenate(sums, axis=1)
            acc_ref[...] += pv
        else:
            m_new = jnp.concatenate(ms, axis=1)
            alpha = jnp.exp2(m_old - m_new)
            l_ref[...] = alpha * l_ref[...] + jnp.concatenate(sums, axis=1)
            m_ref[...] = m_new
            acc_ref[...] = alpha * acc_ref[...] + pv

    n_far = jnp.maximum(i * tq - (tk + MAX_DISTANCE) + tk, 0) // tk

    def attend_all(bounded):
        m_ref[...] = jnp.full(m_ref.shape, NEG, F32)
        l_ref[...] = jnp.zeros(l_ref.shape, F32)
        acc_ref[...] = jnp.zeros(acc_ref.shape, F32)
        lta_ref[...] = logits(0)

        def far_pair(p, carry):
            ltb_ref[...] = logits(2 * p + 1)
            attend_tile(2 * p, lta_ref[...], False, bounded)
            lta_ref[...] = logits(2 * p + 2)
            attend_tile(2 * p + 1, ltb_ref[...], False, bounded)
            return carry

        lax.fori_loop(0, n_far // 2, far_pair, 0)

        @pl.when(n_far % 2 == 1)
        def _():
            attend_tile(n_far - 1, lta_ref[...], False, bounded)

        def near_tile(j, carry):
            attend_tile(j, logits(j), True, bounded)
            return carry

        lax.fori_loop(n_far, nkt, near_tile, 0)

    logit_bound = jnp.sqrt(qn2) * bnd_ref[pl.program_id(0), 0] * NORM_SLACK + bnd_ref[pl.program_id(0), 1]
    safe = logit_bound <= LOGIT_SAFE

    @pl.when(safe)
    def _():
        attend_all(bounded=True)

    @pl.when(jnp.logical_not(safe))
    def _():
        attend_all(bounded=False)

    o_t = acc_ref[...] / l_ref[...]
    for h in range(N_HEADS_A):
        o_h = o_t[:, h * tq:(h + 1) * tq].T.astype(BF16)
        out_ref[:, h * HEAD_DIM:(h + 1) * HEAD_DIM] = jnp.dot(
            o_h, wuv_ref[h], preferred_element_type=F32).astype(out_ref.dtype)


def _t5_bucket_np(n):
    max_exact = N_BUCKETS // 2
    nf = np.maximum(n, 1).astype(np.float32)
    large = max_exact + (np.log(nf / max_exact) / math.log(MAX_DISTANCE / max_exact)
                         * (N_BUCKETS - max_exact)).astype(np.int32)
    large = np.minimum(large, N_BUCKETS - 1)
    return np.where(n < max_exact, n, large)


def _dsa_bias_tiles(t5_table):
    tq, tk = DSA_TQ, DSA_TK
    assert MAX_DISTANCE == 128 and tq == 128 and tk == 256
    sl = np.arange(tk)[:, None]
    tl = np.arange(tq)[None, :]
    idx = np.stack([_t5_bucket_np(np.maximum(tl - sl + off, 0)) for off in (0, 128, 256)])
    table = (t5_table.astype(F32) - t5_table[N_BUCKETS - 1].astype(F32)) * LOG2E
    onehot = (jnp.asarray(idx)[..., None] == jnp.arange(N_BUCKETS)).astype(F32)
    near = jnp.einsum('vstn,nh->vsht', onehot, table, precision=lax.Precision.HIGHEST)
    return near.reshape(3, tk, N_HEADS_A * tq), jnp.max(jnp.abs(table))


def dsa_mixer(aq, pf, c, w_uk, w_uv, t5_table, digit16, name):
    B, L, _ = aq.shape
    tq, tk = DSA_TQ, DSA_TK
    assert L % (2 * DSA_SCORE_TILES * tk) == 0
    nq = L // tq
    topk = min(TOPK_MAX, L // 4)
    assert tk >= topk
    o = KV_RANK
    q_idx = pf[..., o:o + IDX_HEADS * IDX_DIM]
    o += IDX_HEADS * IDX_DIM
    k_idx = pf[..., o:o + IDX_DIM]
    o += IDX_DIM
    w_idx = pf[..., o:o + IDX_HEADS]

    qi = q_idx.reshape(B, nq, tq, IDX_HEADS, IDX_DIM).transpose(0, 1, 3, 2, 4)
    qi = qi.reshape(B, nq, IDX_HEADS * tq, IDX_DIM)
    q_hi = qi.astype(BF16)
    q_lo = (qi - q_hi.astype(F32)).astype(BF16)
    qhl = jnp.concatenate([q_hi, q_lo], axis=-1)
    k_hi = k_idx.astype(BF16)
    k_lo = (k_idx - k_hi.astype(F32)).astype(BF16)
    kcat = jnp.concatenate([k_hi, k_hi, k_lo, k_lo], axis=-1)
    wt = (w_idx * (IDX_DIM ** -0.5 * IDX_HEADS ** -0.5)).reshape(B, nq, tq, IDX_HEADS)
    wt = wt.transpose(0, 1, 3, 2)
    ct = jnp.swapaxes(c, 1, 2)
    bias, bias_max = _dsa_bias_tiles(t5_table)
    c_norm = jnp.sqrt(jnp.max(jnp.sum(jnp.square(c.astype(F32)), axis=-1), axis=-1))
    bounds = jnp.stack([c_norm, jnp.broadcast_to(bias_max, (B,))], axis=-1)

    hq = N_HEADS_A * tq
    return pl.pallas_call(
        functools.partial(_dsa_kernel, topk=topk, digit16=digit16),
        grid=(B, nq),
        in_specs=[
            pl.BlockSpec(memory_space=pltpu.SMEM),
            pl.BlockSpec((None, tq, N_HEADS_A * HEAD_DIM), lambda b, i: (b, i, 0)),
            _resident((N_HEADS_A, HEAD_DIM, KV_RANK), lambda b, i: (0, 0, 0)),
            pl.BlockSpec((None, None, IDX_HEADS * tq, 2 * IDX_DIM), lambda b, i: (b, i, 0, 0)),
            pl.BlockSpec((None, None, IDX_HEADS, tq), lambda b, i: (b, i, 0, 0)),
            _resident((None, L, 4 * IDX_DIM), lambda b, i: (b, 0, 0)),
            _resident((None, L, KV_RANK), lambda b, i: (b, 0, 0)),
            _resident((None, KV_RANK, L), lambda b, i: (b, 0, 0)),
            _resident((3, tk, hq), lambda b, i: (0, 0, 0)),
            _resident((N_HEADS_A, KV_RANK, HEAD_DIM), lambda b, i: (0, 0, 0)),
        ],
        out_specs=pl.BlockSpec((None, tq, N_HEADS_A * HEAD_DIM), lambda b, i: (b, i, 0)),
        out_shape=jax.ShapeDtypeStruct((B, L, N_HEADS_A * HEAD_DIM), BF16),
        scratch_shapes=[
            pltpu.VMEM((L, tq), jnp.int32),
            pltpu.VMEM((L, tq), jnp.int16),
            pltpu.VMEM((IDX_HEADS * tq, 4 * IDX_DIM), BF16),
            pltpu.VMEM((hq, KV_RANK), BF16),
            pltpu.VMEM((KV_RANK, hq), F32),
            pltpu.VMEM((1, hq), F32),
            pltpu.VMEM((1, hq), F32),
            pltpu.VMEM((tk, hq), F32),
            pltpu.VMEM((tk, hq), F32),
        ],
        compiler_params=_cparams(("parallel", "arbitrary")), name=name,
    )(bounds, aq, w_uk.astype(BF16), qhl, wt, kcat, c, ct, bias, w_uv.astype(BF16))


FOX_AUG = 2 * HEAD_DIM
CK_PARTS = 3


def _fox_kernel(st_ref, q_ref, k_ref, v_ref, o_ref, m_ref, acc_ref, sa_ref, sb_ref, *, tq, tk):
    i = pl.program_id(2)
    n_full = (i * tq) // tk
    m_ref[...] = jnp.full(m_ref.shape, NEG, F32)
    acc_ref[...] = jnp.zeros(acc_ref.shape, F32)

    def logits(j):
        k0 = pl.multiple_of(j * tk, tk)
        return lax.dot_general(q_ref[...], k_ref[pl.ds(k0, tk), :], (((1,), (1,)), ((), ())),
                               preferred_element_type=F32)

    def step(j, s, masked):
        k0 = pl.multiple_of(j * tk, tk)
        if masked:
            t_pos = i * tq + lax.broadcasted_iota(jnp.int32, (tq, tk), 0)
            s_pos = k0 + lax.broadcasted_iota(jnp.int32, (tq, tk), 1)
            s = jnp.where(s_pos <= t_pos, s, NEG)
        m_old = m_ref[...]
        m_new = jnp.maximum(m_old, s.max(axis=1, keepdims=True))
        p = jnp.exp2(s - m_new).astype(BF16)
        pv = jnp.dot(p, v_ref[pl.ds(k0, tk), :], preferred_element_type=F32)
        acc_ref[...] = jnp.exp2(m_old - m_new) * acc_ref[...] + pv
        m_ref[...] = m_new

    b, h = pl.program_id(0), pl.program_id(1)
    nq, nk = pl.num_programs(2), k_ref.shape[0] // tk
    q_norm = st_ref[b, h, i]
    gate_q0 = st_ref[b, h, nq + i]
    reach = q_norm * NORM_SLACK

    def skip_scan(j, first):
        k_norm = jnp.maximum(st_ref[b, h, 2 * nq + j], st_ref[b, h, 2 * nq + n_full])
        gap = gate_q0 - st_ref[b, h, 2 * nq + nk + j]
        dead = reach * 2.0 * k_norm + gap < -UNDERFLOW_BITS
        return jnp.where(jnp.logical_and(dead, first == j), j + 1, first)

    first = lax.fori_loop(0, n_full, skip_scan, jnp.int32(0))
    n_live = n_full - first

    sa_ref[...] = logits(first)

    def full_pair(p, carry):
        j = first + 2 * p
        sb_ref[...] = logits(j + 1)
        step(j, sa_ref[...], masked=False)
        sa_ref[...] = logits(j + 2)
        step(j + 1, sb_ref[...], masked=False)
        return carry

    lax.fori_loop(0, n_live // 2, full_pair, 0)

    @pl.when(n_live % 2 == 1)
    def _():
        sb_ref[...] = logits(n_full)
        step(n_full - 1, sa_ref[...], masked=False)
        step(n_full, sb_ref[...], masked=True)

    @pl.when(n_live % 2 == 0)
    def _():
        step(n_full, sa_ref[...], masked=True)

    acc = acc_ref[...]
    o_ref[...] = (acc[:, :HEAD_DIM] / acc[:, HEAD_DIM:HEAD_DIM + 1]).astype(o_ref.dtype)


def _split_bf16(x, parts):
    out = []
    for _ in range(parts):
        p = x.astype(BF16)
        out.append(p)
        x = x - p.astype(F32)
    return out


def fox_mixer(q, k, v, log_f, tq=512, tk=1024):
    B, L, _ = q.shape
    tq = min(tq, L)
    tk = min(tk, L)
    assert tk % tq == 0 and L % tk == 0
    H = N_HEADS_B
    heads = lambda a: a.reshape(B, L, H, HEAD_DIM).transpose(0, 2, 1, 3)
    ck = (jnp.cumsum(log_f, axis=1) * LOG2E).transpose(0, 2, 1)
    ck_parts = jnp.stack(_split_bf16(ck, CK_PARTS), axis=-1)
    pad = lambda n: jnp.zeros((B, H, L, n), BF16)
    q_aug = jnp.concatenate([heads(q), jnp.full((B, H, L, CK_PARTS), -1.0, BF16),
                             pad(FOX_AUG - HEAD_DIM - CK_PARTS)], axis=-1)
    k_aug = jnp.concatenate([heads(k), ck_parts, pad(FOX_AUG - HEAD_DIM - CK_PARTS)], axis=-1)
    v_aug = jnp.concatenate([heads(v), jnp.ones((B, H, L, 1), BF16), pad(FOX_AUG - HEAD_DIM - 1)], axis=-1)

    norm = lambda a: jnp.sqrt(jnp.sum(jnp.square(heads(a).astype(F32)), axis=-1))
    stats = jnp.concatenate([
        norm(q).reshape(B, H, L // tq, tq).max(axis=-1),
        ck[:, :, ::tq],
        norm(k).reshape(B, H, L // tk, tk).max(axis=-1),
        ck[:, :, tk - 1::tk],
    ], axis=-1)

    return pl.pallas_call(
        functools.partial(_fox_kernel, tq=tq, tk=tk),
        grid=(B, H, L // tq),
        in_specs=[
            pl.BlockSpec(memory_space=pltpu.SMEM),
            pl.BlockSpec((None, None, tq, FOX_AUG), lambda b, h, i: (b, h, i, 0)),
            pl.BlockSpec((None, None, L, FOX_AUG), lambda b, h, i: (b, h, 0, 0)),
            pl.BlockSpec((None, None, L, FOX_AUG), lambda b, h, i: (b, h, 0, 0)),
        ],
        out_specs=pl.BlockSpec((None, tq, HEAD_DIM), lambda b, h, i: (b, i, h)),
        out_shape=jax.ShapeDtypeStruct((B, L, H * HEAD_DIM), BF16),
        scratch_shapes=[pltpu.VMEM((tq, 1), F32), pltpu.VMEM((tq, FOX_AUG), F32),
                        pltpu.VMEM((tq, tk), F32), pltpu.VMEM((tq, tk), F32)],
        compiler_params=_cparams(("parallel", "parallel", "arbitrary")), name="fox",
    )(stats, q_aug, k_aug, v_aug)


def _out_proj_kernel(x_ref, oa_ref, ob_ref, wa_ref, wb_ref, g_ref, x2_ref, hn_ref):
    y = x_ref[...] + jnp.dot(oa_ref[...], wa_ref[...], preferred_element_type=F32)
    y = y + jnp.dot(ob_ref[...], wb_ref[...], preferred_element_type=F32)
    x2_ref[...] = y
    hn_ref[...] = _rms(y, g_ref[...]).astype(hn_ref.dtype)


def out_proj(x, oa, ob, w_o, g2, tm=512):
    B, L, D = x.shape
    tm = min(tm, L)
    wa = w_o[:N_HEADS_A * HEAD_DIM].astype(BF16)
    wb = w_o[N_HEADS_A * HEAD_DIM:].astype(BF16)
    row = lambda n: pl.BlockSpec((None, tm, n), lambda b, i: (b, i, 0))
    return pl.pallas_call(
        _out_proj_kernel,
        grid=(B, L // tm),
        in_specs=[row(D), row(oa.shape[-1]), row(ob.shape[-1]),
                  _resident(wa.shape, lambda b, i: (0, 0)), _resident(wb.shape, lambda b, i: (0, 0)),
                  pl.BlockSpec((1, D), lambda b, i: (0, 0))],
        out_specs=[row(D), row(D)],
        out_shape=[jax.ShapeDtypeStruct((B, L, D), F32), jax.ShapeDtypeStruct((B, L, D), BF16)],
        compiler_params=_cparams(("parallel", "parallel")), name="out_proj",
    )(x, oa, ob, wa, wb, g2.reshape(1, D))


FFN_HALO = 16


def _ffn_kernel(h_ref, halo_ref, x2_ref, wg_ref, wv_ref, cwg_ref, cwv_ref, cbg_ref, cbv_ref, wd_ref, fg_ref,
                o_ref, hbuf_ref, ug_ref, uv_ref, acc_ref, *, tm):
    i = pl.program_id(1)
    f = pl.program_id(2)

    @pl.when(f == 0)
    def _():
        hbuf_ref[0:FFN_HALO, :] = jnp.where(i > 0, halo_ref[...], jnp.zeros_like(halo_ref))
        hbuf_ref[FFN_HALO:, :] = h_ref[...]
        acc_ref[...] = jnp.zeros(acc_ref.shape, F32)

    ug_ref[...] = jnp.dot(hbuf_ref[...], wg_ref[...], preferred_element_type=F32)
    uv_ref[...] = jnp.dot(hbuf_ref[...], wv_ref[...], preferred_element_type=F32)

    def conv(u_ref, cw_ref, cb_ref):
        a = cb_ref[...] + cw_ref[0:1, :] * u_ref[pl.ds(FFN_HALO - 2, tm), :]
        a = a + cw_ref[1:2, :] * u_ref[pl.ds(FFN_HALO - 1, tm), :]
        return a + cw_ref[2:3, :] * u_ref[pl.ds(FFN_HALO, tm), :]

    gate = conv(ug_ref, cwg_ref, cbg_ref)
    val = conv(uv_ref, cwv_ref, cbv_ref)
    act = (gate * jax.nn.sigmoid(gate) * val).astype(BF16)
    acc_ref[...] += jnp.dot(act, wd_ref[...], preferred_element_type=F32)

    @pl.when(f == pl.num_programs(2) - 1)
    def _():
        o_ref[...] = _rms(x2_ref[...] + acc_ref[...], fg_ref[...]).astype(o_ref.dtype)


def conv_ffn_final(hn, x2, w_up, conv_w, conv_b, w_down, final_g, tm=512, fc=512):
    B, L, D = hn.shape
    F = w_down.shape[0]
    tm = min(tm, L)
    assert F % fc == 0 and tm % FFN_HALO == 0
    nf = F // fc
    wu = w_up.astype(BF16)
    wd = w_down.astype(BF16)
    cb = conv_b.reshape(1, 2 * F)
    hpt = tm // FFN_HALO
    return pl.pallas_call(
        functools.partial(_ffn_kernel, tm=tm),
        grid=(B, L // tm, nf),
        in_specs=[
            pl.BlockSpec((None, tm, D), lambda b, i, f: (b, i, 0)),
            pl.BlockSpec((None, FFN_HALO, D), lambda b, i, f: (b, jnp.maximum(i * hpt - 1, 0), 0)),
            pl.BlockSpec((None, tm, D), lambda b, i, f: (b, i, 0)),
            pl.BlockSpec((D, fc), lambda b, i, f: (0, f)),
            pl.BlockSpec((D, fc), lambda b, i, f: (0, nf + f)),
            pl.BlockSpec((CONV_WIDTH, fc), lambda b, i, f: (0, f)),
            pl.BlockSpec((CONV_WIDTH, fc), lambda b, i, f: (0, nf + f)),
            pl.BlockSpec((1, fc), lambda b, i, f: (0, f)),
            pl.BlockSpec((1, fc), lambda b, i, f: (0, nf + f)),
            pl.BlockSpec((fc, D), lambda b, i, f: (f, 0)),
            pl.BlockSpec((1, D), lambda b, i, f: (0, 0)),
        ],
        out_specs=pl.BlockSpec((None, tm, D), lambda b, i, f: (b, i, 0)),
        out_shape=jax.ShapeDtypeStruct((B, L, D), F32),
        scratch_shapes=[pltpu.VMEM((tm + FFN_HALO, D), BF16),
                        pltpu.VMEM((tm + FFN_HALO, fc), F32),
                        pltpu.VMEM((tm + FFN_HALO, fc), F32),
                        pltpu.VMEM((tm, D), F32)],
        compiler_params=_cparams(("parallel", "parallel", "arbitrary")), name="conv_ffn",
    )(hn, hn, x2, wu, wu, conv_w, conv_w, cb, cb, wd, final_g.reshape(1, D))


def kernel(x, w_in, kv_norm_g, w_uk, w_uv, t5_table, fgate_b, w_o, norm1_g, norm2_g,
           w_up, conv_w, conv_b, w_down, final_g):
    B, L, D = x.shape
    depth = w_in.shape[0]
    na = N_HEADS_A * HEAD_DIM
    nb = N_HEADS_B * HEAD_DIM
    sizes = (na, KV_RANK, IDX_HEADS * IDX_DIM, IDX_DIM, IDX_HEADS, nb, nb, nb, N_HEADS_B)
    offs = np.concatenate([[0], np.cumsum(sizes)])
    n_f32 = KV_RANK + IDX_HEADS * IDX_DIM + IDX_DIM + IDX_HEADS + N_HEADS_B
    n_f32_pad = -(-n_f32 // 128) * 128
    assert depth == 1, "the final-norm FFN epilogue closes a single-layer stack"
    for l in range(depth):
        w = w_in[l]
        w_bq = w[:, offs[5]:offs[6]] * (HEAD_DIM ** -0.5 * LOG2E)
        w_attn = jnp.concatenate([w[:, offs[0]:offs[1]], w_bq, w[:, offs[6]:offs[8]]], axis=1).astype(BF16)
        w_misc = jnp.concatenate([w[:, offs[1]:offs[5]], w[:, offs[8]:offs[9]],
                                  jnp.zeros((D, n_f32_pad - n_f32), w.dtype)], axis=1).astype(BF16)
        pa = norm_proj(x, norm1_g[l], w_attn, BF16, tm=512)
        pf, c = norm_proj(x, norm1_g[l], w_misc, F32, tm=512, kv_g=kv_norm_g[l])
        o_a = jnp.concatenate([
            dsa_mixer(pa[:1], pf[:1], c[:1], w_uk[l], w_uv[l], t5_table, False, "dsa_bits32"),
            dsa_mixer(pa[1:], pf[1:], c[1:], w_uk[l], w_uv[l], t5_table, True, "dsa_digit16"),
        ], axis=0)
        bf = pf[..., n_f32 - N_HEADS_B:n_f32]
        log_f = jax.nn.log_sigmoid(bf + fgate_b[l].astype(F32))
        o_b = fox_mixer(pa[..., na:na + nb], pa[..., na + nb:na + 2 * nb], pa[..., na + 2 * nb:], log_f)
        x2, hn = out_proj(x, o_a, o_b, w_o[l], norm2_g[l])
        return conv_ffn_final(hn, x2, w_up[l], conv_w[l], conv_b[l], w_down[l], final_g)
```

```python
import functools
import math

import numpy as np
import jax
import jax.numpy as jnp
from jax import lax
from jax.experimental import pallas as pl
from jax.experimental.pallas import tpu as pltpu

HEAD_DIM = 128
N_HEADS_A = 8
N_HEADS_B = 8
KV_RANK = 256
IDX_HEADS = 16
IDX_DIM = 64
TOPK_MAX = 256
N_BUCKETS = 32
MAX_DISTANCE = 128
CONV_WIDTH = 3
EPS = 1e-6

F32 = jnp.float32
BF16 = jnp.bfloat16
INT_MIN = -2 ** 31
NEG = -1e30
LOGIT_SAFE = 60.0
NORM_SLACK = 1.02
UNDERFLOW_BITS = 160.0
LOG2E = math.log2(math.e)

VMEM_LIMIT_BYTES = 56 * 1024 * 1024

DSA_TQ = 128
DSA_TK = 256
IDX_HEAD_GROUP = 4
DSA_SCORE_TILES = 4
DSA_COUNT_TILES = 2


def _cparams(sem):
    return pltpu.CompilerParams(dimension_semantics=sem, vmem_limit_bytes=VMEM_LIMIT_BYTES)


def _resident(block_shape, index_map):
    return pl.BlockSpec(block_shape, index_map, pipeline_mode=pl.Buffered(1))


def _rms(x, g):
    ms = jnp.mean(x * x, axis=-1, keepdims=True)
    return x * lax.rsqrt(ms + EPS) * g


LANES = 128
HALF = LANES // 2
W_IDX_LANE = IDX_DIM
GATE_LANE = IDX_DIM + IDX_HEADS
PROJ_TM = 512


def _bf16_parts(x, parts):
    out = []
    for _ in range(parts):
        p = lax.bitcast_convert_type(lax.bitcast_convert_type(x, jnp.int32) & -65536, F32)
        out.append(p)
        x = x - p
    return out


def _lane_lt(shape, n):
    return lax.broadcasted_iota(jnp.int32, shape, 1) < n


def _proj_index_kernel(x_ref, g_ref, w_ref, kvg_ref, fb_ref,
                       c_ref, ct_ref, qhl_ref, kcat_ref, wt_ref, ck_ref, carry_ref, *, tm):
    i = pl.program_id(1)
    tq = DSA_TQ
    h = _rms(x_ref[...], g_ref[...]).astype(BF16)
    first = _lane_lt((tm, LANES), HALF)

    c = _rms(jnp.dot(h, w_ref[:, 0:KV_RANK], preferred_element_type=F32), kvg_ref[...])
    c_ref[...] = c.astype(BF16)
    ct_ref[...] = c.T.astype(BF16)

    pairs_per_dot = 512 // LANES
    for pair in range(IDX_HEADS // 2):
        if pair % pairs_per_dot == 0:
            c0 = KV_RANK + pair * LANES
            q_wide = jnp.dot(h, w_ref[:, c0:c0 + 512], preferred_element_type=F32)
        u = pair % pairs_per_dot
        q = q_wide[:, u * LANES:(u + 1) * LANES]
        hi, lo = _bf16_parts(q, 2)
        even = jnp.where(first, hi, pltpu.roll(lo, HALF, axis=1)).astype(BF16)
        odd = jnp.where(first, pltpu.roll(hi, HALF, axis=1), lo).astype(BF16)
        for r in range(tm // tq):
            qhl_ref[r, (2 * pair) * tq:(2 * pair + 1) * tq, :] = even[r * tq:(r + 1) * tq, :]
            qhl_ref[r, (2 * pair + 1) * tq:(2 * pair + 2) * tq, :] = odd[r * tq:(r + 1) * tq, :]

    c0 = KV_RANK + IDX_HEADS * IDX_DIM
    y = jnp.dot(h, w_ref[:, c0:c0 + LANES], preferred_element_type=F32)
    hi, lo = _bf16_parts(y, 2)
    kcat_ref[:, 0:LANES] = jnp.where(first, hi, pltpu.roll(hi, HALF, axis=1)).astype(BF16)
    kcat_ref[:, LANES:2 * LANES] = jnp.where(first, lo, pltpu.roll(lo, HALF, axis=1)).astype(BF16)
    for r in range(tm // tq):
        y_t = y[r * tq:(r + 1) * tq, :].T
        wt_ref[r] = y_t[W_IDX_LANE:W_IDX_LANE + IDX_HEADS, :] * (IDX_DIM ** -0.5 * IDX_HEADS ** -0.5)

    z = y + fb_ref[...]
    log_f = (jnp.minimum(z, 0.0) - jnp.log(1.0 + jnp.exp(-jnp.abs(z)))) * LOG2E
    tri = jnp.where(lax.broadcasted_iota(jnp.int32, (tm, tm), 0)
                    >= lax.broadcasted_iota(jnp.int32, (tm, tm), 1), 1.0, 0.0).astype(BF16)
    run = jnp.zeros((tm, LANES), F32)
    for part in _bf16_parts(log_f, CK_PARTS):
        run = run + jnp.dot(tri, part.astype(BF16), preferred_element_type=F32)

    @pl.when(i == 0)
    def _():
        carry_ref[...] = jnp.zeros(carry_ref.shape, F32)

    ck = run + carry_ref[...]
    ck_ref[...] = ck
    carry_ref[...] = ck[tm - 1:tm, :]


def proj_index(x, g, w, kv_g, fgate_b):
    B, L, D = x.shape
    tm, tq = min(PROJ_TM, L), DSA_TQ
    n_cols = KV_RANK + IDX_HEADS * IDX_DIM + LANES
    assert w.shape == (D, n_cols) and tm % tq == 0
    nq = L // tq
    fb = jnp.zeros((1, LANES), F32).at[0, GATE_LANE:GATE_LANE + N_HEADS_B].set(fgate_b.astype(F32))
    row = lambda n, dt: (pl.BlockSpec((None, tm, n), lambda b, i: (b, i, 0)), jax.ShapeDtypeStruct((B, L, n), dt))
    outs = [
        row(KV_RANK, BF16),
        (pl.BlockSpec((None, KV_RANK, tm), lambda b, i: (b, 0, i)), jax.ShapeDtypeStruct((B, KV_RANK, L), BF16)),
        (pl.BlockSpec((None, tm // tq, IDX_HEADS * tq, 2 * IDX_DIM), lambda b, i: (b, i, 0, 0)),
         jax.ShapeDtypeStruct((B, nq, IDX_HEADS * tq, 2 * IDX_DIM), BF16)),
        row(4 * IDX_DIM, BF16),
        (pl.BlockSpec((None, tm // tq, IDX_HEADS, tq), lambda b, i: (b, i, 0, 0)),
         jax.ShapeDtypeStruct((B, nq, IDX_HEADS, tq), F32)),
        row(LANES, F32),
    ]
    return pl.pallas_call(
        functools.partial(_proj_index_kernel, tm=tm),
        grid=(B, L // tm),
        in_specs=[pl.BlockSpec((None, tm, D), lambda b, i: (b, i, 0)),
                  pl.BlockSpec((1, D), lambda b, i: (0, 0)),
                  _resident((D, n_cols), lambda b, i: (0, 0)),
                  pl.BlockSpec((1, KV_RANK), lambda b, i: (0, 0)),
                  pl.BlockSpec((1, LANES), lambda b, i: (0, 0))],
        out_specs=[o[0] for o in outs], out_shape=[o[1] for o in outs],
        scratch_shapes=[pltpu.VMEM((1, LANES), F32)],
        compiler_params=_cparams(("parallel", "arbitrary")), name="proj_index",
    )(x, g.reshape(1, D), w, kv_g.reshape(1, KV_RANK), fb)


def _proj_attn_kernel(x_ref, g_ref, w_ref, ck_ref, aq_ref, qa_ref, ka_ref, va_ref, *, tm):
    h = _rms(x_ref[...], g_ref[...]).astype(BF16)
    na = N_HEADS_A * HEAD_DIM
    for c0 in range(0, na, 512):
        aq_ref[:, c0:c0 + 512] = jnp.dot(h, w_ref[:, c0:c0 + 512], preferred_element_type=F32).astype(BF16)

    lane = lax.broadcasted_iota(jnp.int32, (tm, LANES), 1)
    qa_tail = jnp.where(lane < CK_PARTS, -1.0, 0.0).astype(BF16)
    va_tail = jnp.where(lane < 1, 1.0, 0.0).astype(BF16)
    parts = _bf16_parts(ck_ref[...], CK_PARTS)
    group = 512 // HEAD_DIM
    for dst, base in ((qa_ref, na), (ka_ref, na + N_HEADS_B * HEAD_DIM), (va_ref, na + 2 * N_HEADS_B * HEAD_DIM)):
        for g0 in range(0, N_HEADS_B, group):
            c0 = base + g0 * HEAD_DIM
            y = jnp.dot(h, w_ref[:, c0:c0 + 512], preferred_element_type=F32).astype(BF16)
            for u in range(group):
                dst[g0 + u, :, 0:HEAD_DIM] = y[:, u * HEAD_DIM:(u + 1) * HEAD_DIM]
    for hd in range(N_HEADS_B):
        qa_ref[hd, :, HEAD_DIM:] = qa_tail
        va_ref[hd, :, HEAD_DIM:] = va_tail
        tail = jnp.zeros((tm, LANES), F32)
        for j, p in enumerate(parts):
            moved = pltpu.roll(p, (LANES - GATE_LANE - hd + j) % LANES, axis=1)
            tail = jnp.where(lane == j, moved, tail)
        ka_ref[hd, :, HEAD_DIM:] = tail.astype(BF16)


def proj_attn(x, g, w, ck):
    B, L, D = x.shape
    tm = min(PROJ_TM, L)
    na, H = N_HEADS_A * HEAD_DIM, N_HEADS_B
    aug = (pl.BlockSpec((None, H, tm, FOX_AUG), lambda b, i: (b, 0, i, 0)),
           jax.ShapeDtypeStruct((B, H, L, FOX_AUG), BF16))
    outs = [(pl.BlockSpec((None, tm, na), lambda b, i: (b, i, 0)), jax.ShapeDtypeStruct((B, L, na), BF16)),
            aug, aug, aug]
    return pl.pallas_call(
        functools.partial(_proj_attn_kernel, tm=tm),
        grid=(B, L // tm),
        in_specs=[pl.BlockSpec((None, tm, D), lambda b, i: (b, i, 0)),
                  pl.BlockSpec((1, D), lambda b, i: (0, 0)),
                  _resident(w.shape, lambda b, i: (0, 0)),
                  pl.BlockSpec((None, tm, LANES), lambda b, i: (b, i, 0))],
        out_specs=[o[0] for o in outs], out_shape=[o[1] for o in outs],
        compiler_params=_cparams(("parallel", "parallel")), name="proj_attn",
    )(x, g.reshape(1, D), w, ck)


def _dsa_kernel(bnd_ref, aq_ref, wuk_ref, qhl_ref, wt_ref, kcat_ref, c_ref, ct_ref, bias_ref, wuv_ref,
                out_ref, s_ref, qcat_ref, ql_ref, acc_ref, m_ref, l_ref, lta_ref, ltb_ref, *, topk):
    tq, tk = DSA_TQ, DSA_TK
    i = pl.program_id(1)
    nkt = (i * tq) // tk + 1
    t_pos = i * tq + lax.broadcasted_iota(jnp.int32, (tk, tq), 1)
    s_loc = lax.broadcasted_iota(jnp.int32, (tk, tq), 0)

    scale = HEAD_DIM ** -0.5 * LOG2E
    qn2 = jnp.float32(0.0)
    for h in range(N_HEADS_A):
        ql = jnp.dot(aq_ref[:, h * HEAD_DIM:(h + 1) * HEAD_DIM], wuk_ref[h], preferred_element_type=F32)
        ql = ql * scale
        qn2 = jnp.maximum(qn2, jnp.max(jnp.sum(ql * ql, axis=1, keepdims=True)))
        ql_ref[h * tq:(h + 1) * tq, :] = ql.astype(BF16)

    qcat_ref[...] = jnp.concatenate([qhl_ref[...], qhl_ref[...]], axis=1)

    sub = DSA_SCORE_TILES
    tc = sub * tk
    n_chunks = (i * tq) // tc + 1

    def score_chunk(jc, carry):
        for u in range(sub):
            k0 = pl.multiple_of(jc * tc + u * tk, tk)
            kcat = kcat_ref[pl.ds(k0, tk), :]
            s = jnp.zeros((tk, tq), F32)
            hg = IDX_HEAD_GROUP
            for g in range(IDX_HEADS // hg):
                d = lax.dot_general(kcat, qcat_ref[g * hg * tq:(g + 1) * hg * tq, :],
                                    (((1,), (1,)), ((), ())), preferred_element_type=F32)
                for hh in range(hg):
                    h = g * hg + hh
                    s = s + jnp.maximum(d[:, hh * tq:(hh + 1) * tq], 0.0) * wt_ref[h:h + 1, :]
            s = jnp.where(s == 0.0, 0.0, s)
            bits = lax.bitcast_convert_type(s, jnp.int32)
            key = bits ^ ((bits >> 31) & 0x7FFFFFFF)
            key = jnp.where(k0 + s_loc <= t_pos, key, INT_MIN)
            s_ref[pl.ds(k0, tk), :] = key
        return carry

    lax.fori_loop(0, n_chunks, score_chunk, 0)

    rows = DSA_COUNT_TILES * tk
    n_count = (i * tq) // rows + 1

    def count_ge(cand):
        def body(jc, acc):
            k0 = pl.multiple_of(jc * rows, rows)
            ge = jnp.where(s_ref[pl.ds(k0, rows), :] >= cand, 1, 0).astype(jnp.int32)
            return acc + ge.reshape(rows // 8, 8, tq).sum(axis=0)
        acc = lax.fori_loop(0, n_count, body, jnp.zeros((8, tq), jnp.int32))
        return acc.sum(axis=0, keepdims=True)

    def bit_step(b, carry):
        t, cnt_t = carry
        cand = t + lax.shift_left(jnp.int32(1), 31 - b)
        cnt = count_ge(cand)
        take = cnt >= topk
        return jnp.where(take, cand, t), jnp.where(take, cnt, cnt_t)

    thr, cnt_thr = lax.fori_loop(0, 32, bit_step, (jnp.full((1, tq), INT_MIN, jnp.int32),
                                                   jnp.zeros((1, tq), jnp.int32) + n_count * rows))

    @pl.when(jnp.max(cnt_thr) > topk)
    def _():
        need = (topk - count_ge(thr + 1)).astype(F32)
        tri = jnp.where(lax.broadcasted_iota(jnp.int32, (tk, tk), 0)
                        > lax.broadcasted_iota(jnp.int32, (tk, tk), 1), 1.0, 0.0).astype(BF16)

        def body(j, seen):
            k0 = pl.multiple_of(j * tk, tk)
            x = s_ref[pl.ds(k0, tk), :]
            eq = x == thr
            eqf = jnp.where(eq, 1.0, 0.0)
            rank = jnp.dot(tri, eqf.astype(BF16), preferred_element_type=F32) + seen
            s_ref[pl.ds(k0, tk), :] = jnp.where(eq & (rank >= need), INT_MIN, x)
            return seen + eqf.sum(axis=0, keepdims=True)

        lax.fori_loop(0, nkt, body, jnp.zeros((1, tq), F32))

    def logits(j):
        k0 = pl.multiple_of(j * tk, tk)
        return lax.dot_general(c_ref[pl.ds(k0, tk), :], ql_ref[...],
                               (((1,), (1,)), ((), ())), preferred_element_type=F32)

    def attend_tile(j, lt, near, bounded):
        k0 = pl.multiple_of(j * tk, tk)
        sel = s_ref[pl.ds(k0, tk), :] >= thr
        if near:
            sel = sel & (k0 + s_loc <= t_pos)
            variant = jnp.minimum((i * tq - k0) // MAX_DISTANCE, 2)
            lt = lt + bias_ref[variant]
        ps, ms, sums = [], [], []
        if not bounded:
            m_old = m_ref[...]
        for h in range(N_HEADS_A):
            z = jnp.where(sel, lt[:, h * tq:(h + 1) * tq], NEG)
            if not bounded:
                m_new = jnp.maximum(m_old[:, h * tq:(h + 1) * tq], z.max(axis=0, keepdims=True))
                ms.append(m_new)
                z = z - m_new
            p = jnp.exp2(z)
            sums.append(p.sum(axis=0, keepdims=True))
            ps.append(p.astype(BF16))
        pv = jnp.dot(ct_ref[:, pl.ds(k0, tk)], jnp.concatenate(ps, axis=1), preferred_element_type=F32)
        if bounded:
            l_ref[...] += jnp.concatenate(sums, axis=1)
            acc_ref[...] += pv
        else:
            m_new = jnp.concatenate(ms, axis=1)
            alpha = jnp.exp2(m_old - m_new)
            l_ref[...] = alpha * l_ref[...] + jnp.concatenate(sums, axis=1)
            m_ref[...] = m_new
            acc_ref[...] = alpha * acc_ref[...] + pv

    n_far = jnp.maximum(i * tq - (tk + MAX_DISTANCE) + tk, 0) // tk

    def attend_all(bounded):
        m_ref[...] = jnp.full(m_ref.shape, NEG, F32)
        l_ref[...] = jnp.zeros(l_ref.shape, F32)
        acc_ref[...] = jnp.zeros(acc_ref.shape, F32)
        lta_ref[...] = logits(0)

        def far_pair(p, carry):
            ltb_ref[...] = logits(2 * p + 1)
            attend_tile(2 * p, lta_ref[...], False, bounded)
            lta_ref[...] = logits(2 * p + 2)
            attend_tile(2 * p + 1, ltb_ref[...], False, bounded)
            return carry

        lax.fori_loop(0, n_far // 2, far_pair, 0)

        @pl.when(n_far % 2 == 1)
        def _():
            attend_tile(n_far - 1, lta_ref[...], False, bounded)

        def near_tile(j, carry):
            attend_tile(j, logits(j), True, bounded)
            return carry

        lax.fori_loop(n_far, nkt, near_tile, 0)

    logit_bound = jnp.sqrt(qn2) * bnd_ref[pl.program_id(0), 0] * NORM_SLACK + bnd_ref[pl.program_id(0), 1]
    safe = logit_bound <= LOGIT_SAFE

    @pl.when(safe)
    def _():
        attend_all(bounded=True)

    @pl.when(jnp.logical_not(safe))
    def _():
        attend_all(bounded=False)

    o_t = acc_ref[...] / l_ref[...]
    for h in range(N_HEADS_A):
        o_h = o_t[:, h * tq:(h + 1) * tq].T.astype(BF16)
        out_ref[:, h * HEAD_DIM:(h + 1) * HEAD_DIM] = jnp.dot(
            o_h, wuv_ref[h], preferred_element_type=F32).astype(out_ref.dtype)


def _t5_bucket_np(n):
    max_exact = N_BUCKETS // 2
    nf = np.maximum(n, 1).astype(np.float32)
    large = max_exact + (np.log(nf / max_exact) / math.log(MAX_DISTANCE / max_exact)
                         * (N_BUCKETS - max_exact)).astype(np.int32)
    large = np.minimum(large, N_BUCKETS - 1)
    return np.where(n < max_exact, n, large)


def _dsa_bias_tiles(t5_table):
    tq, tk = DSA_TQ, DSA_TK
    assert MAX_DISTANCE == 128 and tq == 128 and tk == 256
    sl = np.arange(tk)[:, None]
    tl = np.arange(tq)[None, :]
    idx = np.stack([_t5_bucket_np(np.maximum(tl - sl + off, 0)) for off in (0, 128, 256)])
    table = (t5_table.astype(F32) - t5_table[N_BUCKETS - 1].astype(F32)) * LOG2E
    onehot = (jnp.asarray(idx)[..., None] == jnp.arange(N_BUCKETS)).astype(F32)
    near = jnp.einsum('vstn,nh->vsht', onehot, table, precision=lax.Precision.HIGHEST)
    return near.reshape(3, tk, N_HEADS_A * tq), jnp.max(jnp.abs(table))


def dsa_mixer(aq, c, ct, qhl, kcat, wt, w_uk, w_uv, t5_table):
    B, L, _ = aq.shape
    tq, tk = DSA_TQ, DSA_TK
    assert L % (DSA_SCORE_TILES * tk) == 0
    nq = L // tq
    topk = min(TOPK_MAX, L // 4)
    assert tk >= topk
    bias, bias_max = _dsa_bias_tiles(t5_table)
    c_norm = jnp.sqrt(jnp.max(jnp.sum(jnp.square(c.astype(F32)), axis=-1), axis=-1))
    bounds = jnp.stack([c_norm, jnp.broadcast_to(bias_max, (B,))], axis=-1)

    hq = N_HEADS_A * tq
    return pl.pallas_call(
        functools.partial(_dsa_kernel, topk=topk),
        grid=(B, nq),
        in_specs=[
            pl.BlockSpec(memory_space=pltpu.SMEM),
            pl.BlockSpec((None, tq, N_HEADS_A * HEAD_DIM), lambda b, i: (b, i, 0)),
            _resident((N_HEADS_A, HEAD_DIM, KV_RANK), lambda b, i: (0, 0, 0)),
            pl.BlockSpec((None, None, IDX_HEADS * tq, 2 * IDX_DIM), lambda b, i: (b, i, 0, 0)),
            pl.BlockSpec((None, None, IDX_HEADS, tq), lambda b, i: (b, i, 0, 0)),
            _resident((None, L, 4 * IDX_DIM), lambda b, i: (b, 0, 0)),
            _resident((None, L, KV_RANK), lambda b, i: (b, 0, 0)),
            _resident((None, KV_RANK, L), lambda b, i: (b, 0, 0)),
            _resident((3, tk, hq), lambda b, i: (0, 0, 0)),
            _resident((N_HEADS_A, KV_RANK, HEAD_DIM), lambda b, i: (0, 0, 0)),
        ],
        out_specs=pl.BlockSpec((None, tq, N_HEADS_A * HEAD_DIM), lambda b, i: (b, i, 0)),
        out_shape=jax.ShapeDtypeStruct((B, L, N_HEADS_A * HEAD_DIM), BF16),
        scratch_shapes=[
            pltpu.VMEM((L, tq), jnp.int32),
            pltpu.VMEM((IDX_HEADS * tq, 4 * IDX_DIM), BF16),
            pltpu.VMEM((hq, KV_RANK), BF16),
            pltpu.VMEM((KV_RANK, hq), F32),
            pltpu.VMEM((1, hq), F32),
            pltpu.VMEM((1, hq), F32),
            pltpu.VMEM((tk, hq), F32),
            pltpu.VMEM((tk, hq), F32),
        ],
        compiler_params=_cparams(("parallel", "arbitrary")), name="dsa",
    )(bounds, aq, w_uk.astype(BF16), qhl, wt, kcat, c, ct, bias, w_uv.astype(BF16))


FOX_AUG = 2 * HEAD_DIM
CK_PARTS = 3


def _fox_kernel(st_ref, q_ref, k_ref, v_ref, o_ref, m_ref, acc_ref, sa_ref, sb_ref, *, tq, tk):
    i = pl.program_id(2)
    n_full = (i * tq) // tk
    m_ref[...] = jnp.full(m_ref.shape, NEG, F32)
    acc_ref[...] = jnp.zeros(acc_ref.shape, F32)

    def logits(j):
        k0 = pl.multiple_of(j * tk, tk)
        return lax.dot_general(q_ref[...], k_ref[pl.ds(k0, tk), :], (((1,), (1,)), ((), ())),
                               preferred_element_type=F32)

    def step(j, s, masked):
        k0 = pl.multiple_of(j * tk, tk)
        if masked:
            t_pos = i * tq + lax.broadcasted_iota(jnp.int32, (tq, tk), 0)
            s_pos = k0 + lax.broadcasted_iota(jnp.int32, (tq, tk), 1)
            s = jnp.where(s_pos <= t_pos, s, NEG)
        m_old = m_ref[...]
        m_new = jnp.maximum(m_old, s.max(axis=1, keepdims=True))
        p = jnp.exp2(s - m_new).astype(BF16)
        pv = jnp.dot(p, v_ref[pl.ds(k0, tk), :], preferred_element_type=F32)
        acc_ref[...] = jnp.exp2(m_old - m_new) * acc_ref[...] + pv
        m_ref[...] = m_new

    b, h = pl.program_id(0), pl.program_id(1)
    nq, nk = pl.num_programs(2), k_ref.shape[0] // tk
    q_norm = st_ref[b, h, i]
    gate_q0 = st_ref[b, h, nq + i]
    reach = q_norm * NORM_SLACK

    def skip_scan(j, first):
        k_norm = jnp.maximum(st_ref[b, h, 2 * nq + j], st_ref[b, h, 2 * nq + n_full])
        gap = gate_q0 - st_ref[b, h, 2 * nq + nk + j]
        dead = reach * 2.0 * k_norm + gap < -UNDERFLOW_BITS
        return jnp.where(jnp.logical_and(dead, first == j), j + 1, first)

    first = lax.fori_loop(0, n_full, skip_scan, jnp.int32(0))
    n_live = n_full - first

    sa_ref[...] = logits(first)

    def full_pair(p, carry):
        j = first + 2 * p
        sb_ref[...] = logits(j + 1)
        step(j, sa_ref[...], masked=False)
        sa_ref[...] = logits(j + 2)
        step(j + 1, sb_ref[...], masked=False)
        return carry

    lax.fori_loop(0, n_live // 2, full_pair, 0)

    @pl.when(n_live % 2 == 1)
    def _():
        sb_ref[...] = logits(n_full)
        step(n_full - 1, sa_ref[...], masked=False)
        step(n_full, sb_ref[...], masked=True)

    @pl.when(n_live % 2 == 0)
    def _():
        step(n_full, sa_ref[...], masked=True)

    acc = acc_ref[...]
    o_ref[...] = (acc[:, :HEAD_DIM] / acc[:, HEAD_DIM:HEAD_DIM + 1]).astype(o_ref.dtype)


def fox_mixer(q_aug, k_aug, v_aug, ck, tq=512, tk=1024):
    B, H, L, _ = q_aug.shape
    tq = min(tq, L)
    tk = min(tk, L)
    assert tk % tq == 0 and L % tk == 0

    norm = lambda a: jnp.sqrt(jnp.sum(jnp.square(a[..., :HEAD_DIM].astype(F32)), axis=-1))
    gate = lambda rows: rows[:, :, GATE_LANE:GATE_LANE + H].transpose(0, 2, 1)
    stats = jnp.concatenate([
        norm(q_aug).reshape(B, H, L // tq, tq).max(axis=-1),
        gate(ck[:, ::tq]),
        norm(k_aug).reshape(B, H, L // tk, tk).max(axis=-1),
        gate(ck[:, tk - 1::tk]),
    ], axis=-1)

    return pl.pallas_call(
        functools.partial(_fox_kernel, tq=tq, tk=tk),
        grid=(B, H, L // tq),
        in_specs=[
            pl.BlockSpec(memory_space=pltpu.SMEM),
            pl.BlockSpec((None, None, tq, FOX_AUG), lambda b, h, i: (b, h, i, 0)),
            pl.BlockSpec((None, None, L, FOX_AUG), lambda b, h, i: (b, h, 0, 0)),
            pl.BlockSpec((None, None, L, FOX_AUG), lambda b, h, i: (b, h, 0, 0)),
        ],
        out_specs=pl.BlockSpec((None, tq, HEAD_DIM), lambda b, h, i: (b, i, h)),
        out_shape=jax.ShapeDtypeStruct((B, L, H * HEAD_DIM), BF16),
        scratch_shapes=[pltpu.VMEM((tq, 1), F32), pltpu.VMEM((tq, FOX_AUG), F32),
                        pltpu.VMEM((tq, tk), F32), pltpu.VMEM((tq, tk), F32)],
        compiler_params=_cparams(("parallel", "parallel", "arbitrary")), name="fox",
    )(stats, q_aug, k_aug, v_aug)


def _out_proj_kernel(x_ref, oa_ref, ob_ref, wa_ref, wb_ref, g_ref, x2_ref, hn_ref):
    y = x_ref[...] + jnp.dot(oa_ref[...], wa_ref[...], preferred_element_type=F32)
    y = y + jnp.dot(ob_ref[...], wb_ref[...], preferred_element_type=F32)
    x2_ref[...] = y
    hn_ref[...] = _rms(y, g_ref[...]).astype(hn_ref.dtype)


def out_proj(x, oa, ob, w_o, g2, tm=512):
    B, L, D = x.shape
    tm = min(tm, L)
    wa = w_o[:N_HEADS_A * HEAD_DIM].astype(BF16)
    wb = w_o[N_HEADS_A * HEAD_DIM:].astype(BF16)
    row = lambda n: pl.BlockSpec((None, tm, n), lambda b, i: (b, i, 0))
    return pl.pallas_call(
        _out_proj_kernel,
        grid=(B, L // tm),
        in_specs=[row(D), row(oa.shape[-1]), row(ob.shape[-1]),
                  _resident(wa.shape, lambda b, i: (0, 0)), _resident(wb.shape, lambda b, i: (0, 0)),
                  pl.BlockSpec((1, D), lambda b, i: (0, 0))],
        out_specs=[row(D), row(D)],
        out_shape=[jax.ShapeDtypeStruct((B, L, D), F32), jax.ShapeDtypeStruct((B, L, D), BF16)],
        compiler_params=_cparams(("parallel", "parallel")), name="out_proj",
    )(x, oa, ob, wa, wb, g2.reshape(1, D))


FFN_HALO = 16


def _ffn_kernel(h_ref, halo_ref, x2_ref, wg_ref, wv_ref, cwg_ref, cwv_ref, cbg_ref, cbv_ref, wd_ref, fg_ref,
                o_ref, hbuf_ref, ug_ref, uv_ref, acc_ref, *, tm):
    i = pl.program_id(1)
    f = pl.program_id(2)

    @pl.when(f == 0)
    def _():
        hbuf_ref[0:FFN_HALO, :] = jnp.where(i > 0, halo_ref[...], jnp.zeros_like(halo_ref))
        hbuf_ref[FFN_HALO:, :] = h_ref[...]
        acc_ref[...] = jnp.zeros(acc_ref.shape, F32)

    ug_ref[...] = jnp.dot(hbuf_ref[...], wg_ref[...], preferred_element_type=F32)
    uv_ref[...] = jnp.dot(hbuf_ref[...], wv_ref[...], preferred_element_type=F32)

    def conv(u_ref, cw_ref, cb_ref):
        a = cb_ref[...] + cw_ref[0:1, :] * u_ref[pl.ds(FFN_HALO - 2, tm), :]
        a = a + cw_ref[1:2, :] * u_ref[pl.ds(FFN_HALO - 1, tm), :]
        return a + cw_ref[2:3, :] * u_ref[pl.ds(FFN_HALO, tm), :]

    gate = conv(ug_ref, cwg_ref, cbg_ref)
    val = conv(uv_ref, cwv_ref, cbv_ref)
    act = (gate * jax.nn.sigmoid(gate) * val).astype(BF16)
    acc_ref[...] += jnp.dot(act, wd_ref[...], preferred_element_type=F32)

    @pl.when(f == pl.num_programs(2) - 1)
    def _():
        o_ref[...] = _rms(x2_ref[...] + acc_ref[...], fg_ref[...]).astype(o_ref.dtype)


def conv_ffn_final(hn, x2, w_up, conv_w, conv_b, w_down, final_g, tm=512, fc=512):
    B, L, D = hn.shape
    F = w_down.shape[0]
    tm = min(tm, L)
    assert F % fc == 0 and tm % FFN_HALO == 0
    nf = F // fc
    wu = w_up.astype(BF16)
    wd = w_down.astype(BF16)
    cb = conv_b.reshape(1, 2 * F)
    hpt = tm // FFN_HALO
    return pl.pallas_call(
        functools.partial(_ffn_kernel, tm=tm),
        grid=(B, L // tm, nf),
        in_specs=[
            pl.BlockSpec((None, tm, D), lambda b, i, f: (b, i, 0)),
            pl.BlockSpec((None, FFN_HALO, D), lambda b, i, f: (b, jnp.maximum(i * hpt - 1, 0), 0)),
            pl.BlockSpec((None, tm, D), lambda b, i, f: (b, i, 0)),
            pl.BlockSpec((D, fc), lambda b, i, f: (0, f)),
            pl.BlockSpec((D, fc), lambda b, i, f: (0, nf + f)),
            pl.BlockSpec((CONV_WIDTH, fc), lambda b, i, f: (0, f)),
            pl.BlockSpec((CONV_WIDTH, fc), lambda b, i, f: (0, nf + f)),
            pl.BlockSpec((1, fc), lambda b, i, f: (0, f)),
            pl.BlockSpec((1, fc), lambda b, i, f: (0, nf + f)),
            pl.BlockSpec((fc, D), lambda b, i, f: (f, 0)),
            pl.BlockSpec((1, D), lambda b, i, f: (0, 0)),
        ],
        out_specs=pl.BlockSpec((None, tm, D), lambda b, i, f: (b, i, 0)),
        out_shape=jax.ShapeDtypeStruct((B, L, D), F32),
        scratch_shapes=[pltpu.VMEM((tm + FFN_HALO, D), BF16),
                        pltpu.VMEM((tm + FFN_HALO, fc), F32),
                        pltpu.VMEM((tm + FFN_HALO, fc), F32),
                        pltpu.VMEM((tm, D), F32)],
        compiler_params=_cparams(("parallel", "parallel", "arbitrary")), name="conv_ffn",
    )(hn, hn, x2, wu, wu, conv_w, conv_w, cb, cb, wd, final_g.reshape(1, D))


def kernel(x, w_in, kv_norm_g, w_uk, w_uv, t5_table, fgate_b, w_o, norm1_g, norm2_g,
           w_up, conv_w, conv_b, w_down, final_g):
    B, L, D = x.shape
    depth = w_in.shape[0]
    na = N_HEADS_A * HEAD_DIM
    nb = N_HEADS_B * HEAD_DIM
    sizes = (na, KV_RANK, IDX_HEADS * IDX_DIM, IDX_DIM, IDX_HEADS, nb, nb, nb, N_HEADS_B)
    offs = np.concatenate([[0], np.cumsum(sizes)])
    assert depth == 1, "the final-norm FFN epilogue closes a single-layer stack"
    assert GATE_LANE + N_HEADS_B <= LANES
    for l in range(depth):
        w = w_in[l]
        w_bq = w[:, offs[5]:offs[6]] * (HEAD_DIM ** -0.5 * LOG2E)
        w_attn = jnp.concatenate([w[:, offs[0]:offs[1]], w_bq, w[:, offs[6]:offs[8]]], axis=1).astype(BF16)
        w_index = jnp.concatenate([w[:, offs[1]:offs[5]], w[:, offs[8]:offs[9]],
                                   jnp.zeros((D, LANES - GATE_LANE - N_HEADS_B), w.dtype)], axis=1).astype(BF16)
        c, ct, qhl, kcat, wt, ck = proj_index(x, norm1_g[l], w_index, kv_norm_g[l], fgate_b[l])
        aq, q_aug, k_aug, v_aug = proj_attn(x, norm1_g[l], w_attn, ck)
        o_a = dsa_mixer(aq, c, ct, qhl, kcat, wt, w_uk[l], w_uv[l], t5_table)
        o_b = fox_mixer(q_aug, k_aug, v_aug, ck)
        x2, hn = out_proj(x, o_a, o_b, w_o[l], norm2_g[l])
        return conv_ffn_final(hn, x2, w_up[l], conv_w[l], conv_b[l], w_down[l], final_g)
```

```python
import functools
import math

import numpy as np
import jax
import jax.numpy as jnp
from jax import lax
from jax.experimental import pallas as pl
from jax.experimental.pallas import tpu as pltpu

HEAD_DIM = 128
N_HEADS_A = 8
N_HEADS_B = 8
KV_RANK = 256
IDX_HEADS = 16
IDX_DIM = 64
TOPK_MAX = 256
N_BUCKETS = 32
MAX_DISTANCE = 128
CONV_WIDTH = 3
EPS = 1e-6

F32 = jnp.float32
BF16 = jnp.bfloat16
INT_MIN = -2 ** 31
NEG = -1e30
LOGIT_SAFE = 60.0
NORM_SLACK = 1.02
UNDERFLOW_BITS = 160.0
LOG2E = math.log2(math.e)

VMEM_LIMIT_BYTES = 56 * 1024 * 1024

DSA_TQ = 128
DSA_TK = 256
IDX_HEAD_GROUP = 4
DSA_SCORE_TILES = 4
DSA_COUNT_TILES = 2
DSA_COUNT_PARTS = 2
DSA_BIT_STAGES = (0, 22, 24, 26, 28, 30, 32)


def _cparams(sem):
    return pltpu.CompilerParams(dimension_semantics=sem, vmem_limit_bytes=VMEM_LIMIT_BYTES)


def _resident(block_shape, index_map):
    return pl.BlockSpec(block_shape, index_map, pipeline_mode=pl.Buffered(1))


def _rms(x, g):
    ms = jnp.mean(x * x, axis=-1, keepdims=True)
    return x * lax.rsqrt(ms + EPS) * g


LANES = 128
HALF = LANES // 2
W_IDX_LANE = IDX_DIM
GATE_LANE = IDX_DIM + IDX_HEADS
PROJ_TM = 512


def _bf16_parts(x, parts):
    out = []
    for _ in range(parts):
        p = lax.bitcast_convert_type(lax.bitcast_convert_type(x, jnp.int32) & -65536, F32)
        out.append(p)
        x = x - p
    return out


def _lane_lt(shape, n):
    return lax.broadcasted_iota(jnp.int32, shape, 1) < n


def _proj_index_kernel(x_ref, g_ref, w_ref, kvg_ref, fb_ref,
                       c_ref, ct_ref, qhl_ref, kcat_ref, wt_ref, ck_ref, carry_ref, *, tm):
    i = pl.program_id(1)
    tq = DSA_TQ
    h = _rms(x_ref[...], g_ref[...]).astype(BF16)
    first = _lane_lt((tm, LANES), HALF)

    c = _rms(jnp.dot(h, w_ref[:, 0:KV_RANK], preferred_element_type=F32), kvg_ref[...])
    c_ref[...] = c.astype(BF16)
    ct_ref[...] = c.T.astype(BF16)

    pairs_per_dot = 512 // LANES
    for pair in range(IDX_HEADS // 2):
        if pair % pairs_per_dot == 0:
            c0 = KV_RANK + pair * LANES
            q_wide = jnp.dot(h, w_ref[:, c0:c0 + 512], preferred_element_type=F32)
        u = pair % pairs_per_dot
        q = q_wide[:, u * LANES:(u + 1) * LANES]
        hi, lo = _bf16_parts(q, 2)
        even = jnp.where(first, hi, pltpu.roll(lo, HALF, axis=1)).astype(BF16)
        odd = jnp.where(first, pltpu.roll(hi, HALF, axis=1), lo).astype(BF16)
        for r in range(tm // tq):
            qhl_ref[r, (2 * pair) * tq:(2 * pair + 1) * tq, :] = even[r * tq:(r + 1) * tq, :]
            qhl_ref[r, (2 * pair + 1) * tq:(2 * pair + 2) * tq, :] = odd[r * tq:(r + 1) * tq, :]

    c0 = KV_RANK + IDX_HEADS * IDX_DIM
    y = jnp.dot(h, w_ref[:, c0:c0 + LANES], preferred_element_type=F32)
    hi, lo = _bf16_parts(y, 2)
    kcat_ref[:, 0:LANES] = jnp.where(first, hi, pltpu.roll(hi, HALF, axis=1)).astype(BF16)
    kcat_ref[:, LANES:2 * LANES] = jnp.where(first, lo, pltpu.roll(lo, HALF, axis=1)).astype(BF16)
    for r in range(tm // tq):
        y_t = y[r * tq:(r + 1) * tq, :].T
        wt_ref[r] = y_t[W_IDX_LANE:W_IDX_LANE + IDX_HEADS, :] * (IDX_DIM ** -0.5 * IDX_HEADS ** -0.5)

    z = y + fb_ref[...]
    log_f = (jnp.minimum(z, 0.0) - jnp.log(1.0 + jnp.exp(-jnp.abs(z)))) * LOG2E
    tri = jnp.where(lax.broadcasted_iota(jnp.int32, (tm, tm), 0)
                    >= lax.broadcasted_iota(jnp.int32, (tm, tm), 1), 1.0, 0.0).astype(BF16)
    run = jnp.zeros((tm, LANES), F32)
    for part in _bf16_parts(log_f, CK_PARTS):
        run = run + jnp.dot(tri, part.astype(BF16), preferred_element_type=F32)

    @pl.when(i == 0)
    def _():
        carry_ref[...] = jnp.zeros(carry_ref.shape, F32)

    ck = run + carry_ref[...]
    ck_ref[...] = ck
    carry_ref[...] = ck[tm - 1:tm, :]


def proj_index(x, g, w, kv_g, fgate_b):
    B, L, D = x.shape
    tm, tq = min(PROJ_TM, L), DSA_TQ
    n_cols = KV_RANK + IDX_HEADS * IDX_DIM + LANES
    assert w.shape == (D, n_cols) and tm % tq == 0
    nq = L // tq
    fb = jnp.zeros((1, LANES), F32).at[0, GATE_LANE:GATE_LANE + N_HEADS_B].set(fgate_b.astype(F32))
    row = lambda n, dt: (pl.BlockSpec((None, tm, n), lambda b, i: (b, i, 0)), jax.ShapeDtypeStruct((B, L, n), dt))
    outs = [
        row(KV_RANK, BF16),
        (pl.BlockSpec((None, KV_RANK, tm), lambda b, i: (b, 0, i)), jax.ShapeDtypeStruct((B, KV_RANK, L), BF16)),
        (pl.BlockSpec((None, tm // tq, IDX_HEADS * tq, 2 * IDX_DIM), lambda b, i: (b, i, 0, 0)),
         jax.ShapeDtypeStruct((B, nq, IDX_HEADS * tq, 2 * IDX_DIM), BF16)),
        row(4 * IDX_DIM, BF16),
        (pl.BlockSpec((None, tm // tq, IDX_HEADS, tq), lambda b, i: (b, i, 0, 0)),
         jax.ShapeDtypeStruct((B, nq, IDX_HEADS, tq), F32)),
        row(LANES, F32),
    ]
    return pl.pallas_call(
        functools.partial(_proj_index_kernel, tm=tm),
        grid=(B, L // tm),
        in_specs=[pl.BlockSpec((None, tm, D), lambda b, i: (b, i, 0)),
                  pl.BlockSpec((1, D), lambda b, i: (0, 0)),
                  _resident((D, n_cols), lambda b, i: (0, 0)),
                  pl.BlockSpec((1, KV_RANK), lambda b, i: (0, 0)),
                  pl.BlockSpec((1, LANES), lambda b, i: (0, 0))],
        out_specs=[o[0] for o in outs], out_shape=[o[1] for o in outs],
        scratch_shapes=[pltpu.VMEM((1, LANES), F32)],
        compiler_params=_cparams(("parallel", "arbitrary")), name="proj_index",
    )(x, g.reshape(1, D), w, kv_g.reshape(1, KV_RANK), fb)


def _proj_attn_kernel(x_ref, g_ref, w_ref, ck_ref, aq_ref, qa_ref, ka_ref, va_ref, *, tm):
    h = _rms(x_ref[...], g_ref[...]).astype(BF16)
    na = N_HEADS_A * HEAD_DIM
    for c0 in range(0, na, 512):
        aq_ref[:, c0:c0 + 512] = jnp.dot(h, w_ref[:, c0:c0 + 512], preferred_element_type=F32).astype(BF16)

    lane = lax.broadcasted_iota(jnp.int32, (tm, LANES), 1)
    qa_tail = jnp.where(lane < CK_PARTS, -1.0, 0.0).astype(BF16)
    va_tail = jnp.where(lane < 1, 1.0, 0.0).astype(BF16)
    parts = _bf16_parts(ck_ref[...], CK_PARTS)
    group = 512 // HEAD_DIM
    for dst, base in ((qa_ref, na), (ka_ref, na + N_HEADS_B * HEAD_DIM), (va_ref, na + 2 * N_HEADS_B * HEAD_DIM)):
        for g0 in range(0, N_HEADS_B, group):
            c0 = base + g0 * HEAD_DIM
            y = jnp.dot(h, w_ref[:, c0:c0 + 512], preferred_element_type=F32).astype(BF16)
            for u in range(group):
                dst[g0 + u, :, 0:HEAD_DIM] = y[:, u * HEAD_DIM:(u + 1) * HEAD_DIM]
    for hd in range(N_HEADS_B):
        qa_ref[hd, :, HEAD_DIM:] = qa_tail
        va_ref[hd, :, HEAD_DIM:] = va_tail
        tail = jnp.zeros((tm, LANES), F32)
        for j, p in enumerate(parts):
            moved = pltpu.roll(p, (LANES - GATE_LANE - hd + j) % LANES, axis=1)
            tail = jnp.where(lane == j, moved, tail)
        ka_ref[hd, :, HEAD_DIM:] = tail.astype(BF16)


def proj_attn(x, g, w, ck):
    B, L, D = x.shape
    tm = min(PROJ_TM, L)
    na, H = N_HEADS_A * HEAD_DIM, N_HEADS_B
    aug = (pl.BlockSpec((None, H, tm, FOX_AUG), lambda b, i: (b, 0, i, 0)),
           jax.ShapeDtypeStruct((B, H, L, FOX_AUG), BF16))
    outs = [(pl.BlockSpec((None, tm, na), lambda b, i: (b, i, 0)), jax.ShapeDtypeStruct((B, L, na), BF16)),
            aug, aug, aug]
    return pl.pallas_call(
        functools.partial(_proj_attn_kernel, tm=tm),
        grid=(B, L // tm),
        in_specs=[pl.BlockSpec((None, tm, D), lambda b, i: (b, i, 0)),
                  pl.BlockSpec((1, D), lambda b, i: (0, 0)),
                  _resident(w.shape, lambda b, i: (0, 0)),
                  pl.BlockSpec((None, tm, LANES), lambda b, i: (b, i, 0))],
        out_specs=[o[0] for o in outs], out_shape=[o[1] for o in outs],
        compiler_params=_cparams(("parallel", "parallel")), name="proj_attn",
    )(x, g.reshape(1, D), w, ck)


def _dsa_kernel(bnd_ref, aq_ref, wuk_ref, qhl_ref, wt_ref, kcat_ref, c_ref, ct_ref, bias_ref, wuv_ref,
                out_ref, s_ref, sel_ref, qcat_ref, ql_ref, acc_ref, m_ref, l_ref, lta_ref, ltb_ref, *, topk):
    tq, tk = DSA_TQ, DSA_TK
    i = pl.program_id(1)
    nkt = (i * tq) // tk + 1
    t_pos = i * tq + lax.broadcasted_iota(jnp.int32, (tk, tq), 1)
    s_loc = lax.broadcasted_iota(jnp.int32, (tk, tq), 0)

    scale = HEAD_DIM ** -0.5 * LOG2E
    qn2 = jnp.float32(0.0)
    for h in range(N_HEADS_A):
        ql = jnp.dot(aq_ref[:, h * HEAD_DIM:(h + 1) * HEAD_DIM], wuk_ref[h], preferred_element_type=F32)
        ql = ql * scale
        qn2 = jnp.maximum(qn2, jnp.max(jnp.sum(ql * ql, axis=1, keepdims=True)))
        ql_ref[h * tq:(h + 1) * tq, :] = ql.astype(BF16)

    qcat_ref[...] = jnp.concatenate([qhl_ref[...], qhl_ref[...]], axis=1)

    sub = DSA_SCORE_TILES
    tc = sub * tk
    n_chunks = (i * tq) // tc + 1

    def score_chunk(jc, carry):
        for u in range(sub):
            k0 = pl.multiple_of(jc * tc + u * tk, tk)
            kcat = kcat_ref[pl.ds(k0, tk), :]
            s = jnp.zeros((tk, tq), F32)
            hg = IDX_HEAD_GROUP
            for g in range(IDX_HEADS // hg):
                d = lax.dot_general(kcat, qcat_ref[g * hg * tq:(g + 1) * hg * tq, :],
                                    (((1,), (1,)), ((), ())), preferred_element_type=F32)
                for hh in range(hg):
                    h = g * hg + hh
                    s = s + jnp.maximum(d[:, hh * tq:(hh + 1) * tq], 0.0) * wt_ref[h:h + 1, :]
            s = jnp.where(s == 0.0, 0.0, s)
            bits = lax.bitcast_convert_type(s, jnp.int32)
            key = bits ^ ((bits >> 31) & 0x7FFFFFFF)
            key = jnp.where(k0 + s_loc <= t_pos, key, INT_MIN)
            s_ref[pl.ds(k0, tk), :] = key
        return carry

    lax.fori_loop(0, n_chunks, score_chunk, 0)

    part = DSA_COUNT_TILES * tk
    rows = DSA_COUNT_PARTS * part
    n_count = (i * tq) // rows + 1

    def count_ge(cand):
        def body(jc, acc):
            for u in range(DSA_COUNT_PARTS):
                k0 = pl.multiple_of(jc * rows + u * part, part)
                ge = jnp.where(s_ref[pl.ds(k0, part), :] >= cand, 1, 0).astype(jnp.int32)
                acc = acc + ge.reshape(part // 8, 8, tq).sum(axis=0)
            return acc
        acc = lax.fori_loop(0, n_count, body, jnp.zeros((8, tq), jnp.int32))
        return acc.sum(axis=0, keepdims=True)

    def bit_step(b, carry):
        t, cnt_t = carry
        cand = t + lax.shift_left(jnp.int32(1), 31 - b)
        cnt = count_ge(cand)
        take = cnt >= topk
        return jnp.where(take, cand, t), jnp.where(take, cnt, cnt_t)

    sel_ref[0:1, :] = jnp.full((1, tq), INT_MIN, jnp.int32)
    sel_ref[1:2, :] = jnp.zeros((1, tq), jnp.int32) + n_count * rows
    def bit_stage(b0, b1):
        t, cnt_t = lax.fori_loop(b0, b1, bit_step, (sel_ref[0:1, :], sel_ref[1:2, :]))
        sel_ref[0:1, :] = t
        sel_ref[1:2, :] = cnt_t

    bit_stage(DSA_BIT_STAGES[0], DSA_BIT_STAGES[1])
    for b0, b1 in zip(DSA_BIT_STAGES[1:-1], DSA_BIT_STAGES[2:]):
        pl.when(jnp.max(sel_ref[1:2, :]) > topk)(functools.partial(bit_stage, b0, b1))
    thr, cnt_thr = sel_ref[0:1, :], sel_ref[1:2, :]

    @pl.when(jnp.max(cnt_thr) > topk)
    def _():
        need = (topk - count_ge(thr + 1)).astype(F32)
        tri = jnp.where(lax.broadcasted_iota(jnp.int32, (tk, tk), 0)
                        > lax.broadcasted_iota(jnp.int32, (tk, tk), 1), 1.0, 0.0).astype(BF16)

        def body(j, seen):
            k0 = pl.multiple_of(j * tk, tk)
            x = s_ref[pl.ds(k0, tk), :]
            eq = x == thr
            eqf = jnp.where(eq, 1.0, 0.0)
            rank = jnp.dot(tri, eqf.astype(BF16), preferred_element_type=F32) + seen
            s_ref[pl.ds(k0, tk), :] = jnp.where(eq & (rank >= need), INT_MIN, x)
            return seen + eqf.sum(axis=0, keepdims=True)

        lax.fori_loop(0, nkt, body, jnp.zeros((1, tq), F32))

    def logits(j):
        k0 = pl.multiple_of(j * tk, tk)
        return lax.dot_general(c_ref[pl.ds(k0, tk), :], ql_ref[...],
                               (((1,), (1,)), ((), ())), preferred_element_type=F32)

    def attend_tile(j, lt, near, bounded):
        k0 = pl.multiple_of(j * tk, tk)
        sel = s_ref[pl.ds(k0, tk), :] >= thr
        if near:
            sel = sel & (k0 + s_loc <= t_pos)
            variant = jnp.minimum((i * tq - k0) // MAX_DISTANCE, 2)
            lt = lt + bias_ref[variant]
        ps, ms, sums = [], [], []
        if not bounded:
            m_old = m_ref[...]
        for h in range(N_HEADS_A):
            z = jnp.where(sel, lt[:, h * tq:(h + 1) * tq], NEG)
            if not bounded:
                m_new = jnp.maximum(m_old[:, h * tq:(h + 1) * tq], z.max(axis=0, keepdims=True))
                ms.append(m_new)
                z = z - m_new
            p = jnp.exp2(z)
            sums.append(p.sum(axis=0, keepdims=True))
            ps.append(p.astype(BF16))
        pv = jnp.dot(ct_ref[:, pl.ds(k0, tk)], jnp.concatenate(ps, axis=1), preferred_element_type=F32)
        if bounded:
            l_ref[...] += jnp.concatenate(sums, axis=1)
            acc_ref[...] += pv
        else:
            m_new = jnp.concatenate(ms, axis=1)
            alpha = jnp.exp2(m_old - m_new)
            l_ref[...] = alpha * l_ref[...] + jnp.concatenate(sums, axis=1)
            m_ref[...] = m_new
            acc_ref[...] = alpha * acc_ref[...] + pv

    n_far = jnp.maximum(i * tq - (tk + MAX_DISTANCE) + tk, 0) // tk

    def attend_all(bounded):
        m_ref[...] = jnp.full(m_ref.shape, NEG, F32)
        l_ref[...] = jnp.zeros(l_ref.shape, F32)
        acc_ref[...] = jnp.zeros(acc_ref.shape, F32)
        lta_ref[...] = logits(0)

        def far_pair(p, carry):
            ltb_ref[...] = logits(2 * p + 1)
            attend_tile(2 * p, lta_ref[...], False, bounded)
            lta_ref[...] = logits(2 * p + 2)
            attend_tile(2 * p + 1, ltb_ref[...], False, bounded)
            return carry

        lax.fori_loop(0, n_far // 2, far_pair, 0)

        @pl.when(n_far % 2 == 1)
        def _():
            attend_tile(n_far - 1, lta_ref[...], False, bounded)

        def near_tile(j, carry):
            attend_tile(j, logits(j), True, bounded)
            return carry

        lax.fori_loop(n_far, nkt, near_tile, 0)

    logit_bound = jnp.sqrt(qn2) * bnd_ref[pl.program_id(0), 0] * NORM_SLACK + bnd_ref[pl.program_id(0), 1]
    safe = logit_bound <= LOGIT_SAFE

    @pl.when(safe)
    def _():
        attend_all(bounded=True)

    @pl.when(jnp.logical_not(safe))
    def _():
        attend_all(bounded=False)

    o_t = acc_ref[...] / l_ref[...]
    for h in range(N_HEADS_A):
        o_h = o_t[:, h * tq:(h + 1) * tq].T.astype(BF16)
        out_ref[:, h * HEAD_DIM:(h + 1) * HEAD_DIM] = jnp.dot(
            o_h, wuv_ref[h], preferred_element_type=F32).astype(out_ref.dtype)


def _t5_bucket_np(n):
    max_exact = N_BUCKETS // 2
    nf = np.maximum(n, 1).astype(np.float32)
    large = max_exact + (np.log(nf / max_exact) / math.log(MAX_DISTANCE / max_exact)
                         * (N_BUCKETS - max_exact)).astype(np.int32)
    large = np.minimum(large, N_BUCKETS - 1)
    return np.where(n < max_exact, n, large)


def _dsa_bias_tiles(t5_table):
    tq, tk = DSA_TQ, DSA_TK
    assert MAX_DISTANCE == 128 and tq == 128 and tk == 256
    sl = np.arange(tk)[:, None]
    tl = np.arange(tq)[None, :]
    idx = np.stack([_t5_bucket_np(np.maximum(tl - sl + off, 0)) for off in (0, 128, 256)])
    table = (t5_table.astype(F32) - t5_table[N_BUCKETS - 1].astype(F32)) * LOG2E
    onehot = (jnp.asarray(idx)[..., None] == jnp.arange(N_BUCKETS)).astype(F32)
    near = jnp.einsum('vstn,nh->vsht', onehot, table, precision=lax.Precision.HIGHEST)
    return near.reshape(3, tk, N_HEADS_A * tq), jnp.max(jnp.abs(table))


def dsa_mixer(aq, c, ct, qhl, kcat, wt, w_uk, w_uv, t5_table):
    B, L, _ = aq.shape
    tq, tk = DSA_TQ, DSA_TK
    assert L % (DSA_SCORE_TILES * tk) == 0
    nq = L // tq
    topk = min(TOPK_MAX, L // 4)
    assert tk >= topk
    bias, bias_max = _dsa_bias_tiles(t5_table)
    c_norm = jnp.sqrt(jnp.max(jnp.sum(jnp.square(c.astype(F32)), axis=-1), axis=-1))
    bounds = jnp.stack([c_norm, jnp.broadcast_to(bias_max, (B,))], axis=-1)

    hq = N_HEADS_A * tq
    return pl.pallas_call(
        functools.partial(_dsa_kernel, topk=topk),
        grid=(B, nq),
        in_specs=[
            pl.BlockSpec(memory_space=pltpu.SMEM),
            pl.BlockSpec((None, tq, N_HEADS_A * HEAD_DIM), lambda b, i: (b, i, 0)),
            _resident((N_HEADS_A, HEAD_DIM, KV_RANK), lambda b, i: (0, 0, 0)),
            pl.BlockSpec((None, None, IDX_HEADS * tq, 2 * IDX_DIM), lambda b, i: (b, i, 0, 0)),
            pl.BlockSpec((None, None, IDX_HEADS, tq), lambda b, i: (b, i, 0, 0)),
            _resident((None, L, 4 * IDX_DIM), lambda b, i: (b, 0, 0)),
            _resident((None, L, KV_RANK), lambda b, i: (b, 0, 0)),
            _resident((None, KV_RANK, L), lambda b, i: (b, 0, 0)),
            _resident((3, tk, hq), lambda b, i: (0, 0, 0)),
            _resident((N_HEADS_A, KV_RANK, HEAD_DIM), lambda b, i: (0, 0, 0)),
        ],
        out_specs=pl.BlockSpec((None, tq, N_HEADS_A * HEAD_DIM), lambda b, i: (b, i, 0)),
        out_shape=jax.ShapeDtypeStruct((B, L, N_HEADS_A * HEAD_DIM), BF16),
        scratch_shapes=[
            pltpu.VMEM((L, tq), jnp.int32),
            pltpu.VMEM((2, tq), jnp.int32),
            pltpu.VMEM((IDX_HEADS * tq, 4 * IDX_DIM), BF16),
            pltpu.VMEM((hq, KV_RANK), BF16),
            pltpu.VMEM((KV_RANK, hq), F32),
            pltpu.VMEM((1, hq), F32),
            pltpu.VMEM((1, hq), F32),
            pltpu.VMEM((tk, hq), F32),
            pltpu.VMEM((tk, hq), F32),
        ],
        compiler_params=_cparams(("parallel", "arbitrary")), name="dsa",
    )(bounds, aq, w_uk.astype(BF16), qhl, wt, kcat, c, ct, bias, w_uv.astype(BF16))


FOX_AUG = 2 * HEAD_DIM
CK_PARTS = 3


def _fox_kernel(st_ref, q_ref, k_ref, v_ref, o_ref, m_ref, acc_ref, sa_ref, sb_ref, *, tq, tk):
    i = pl.program_id(2)
    n_full = (i * tq) // tk
    m_ref[...] = jnp.full(m_ref.shape, NEG, F32)
    acc_ref[...] = jnp.zeros(acc_ref.shape, F32)

    def logits(j):
        k0 = pl.multiple_of(j * tk, tk)
        return lax.dot_general(q_ref[...], k_ref[pl.ds(k0, tk), :], (((1,), (1,)), ((), ())),
                               preferred_element_type=F32)

    def step(j, s, masked):
        k0 = pl.multiple_of(j * tk, tk)
        if masked:
            t_pos = i * tq + lax.broadcasted_iota(jnp.int32, (tq, tk), 0)
            s_pos = k0 + lax.broadcasted_iota(jnp.int32, (tq, tk), 1)
            s = jnp.where(s_pos <= t_pos, s, NEG)
        m_old = m_ref[...]
        m_new = jnp.maximum(m_old, s.max(axis=1, keepdims=True))
        p = jnp.exp2(s - m_new).astype(BF16)
        pv = jnp.dot(p, v_ref[pl.ds(k0, tk), :], preferred_element_type=F32)
        acc_ref[...] = jnp.exp2(m_old - m_new) * acc_ref[...] + pv
        m_ref[...] = m_new

    b, h = pl.program_id(0), pl.program_id(1)
    nq, nk = pl.num_programs(2), k_ref.shape[0] // tk
    q_norm = st_ref[b, h, i]
    gate_q0 = st_ref[b, h, nq + i]
    reach = q_norm * NORM_SLACK

    def skip_scan(j, first):
        k_norm = jnp.maximum(st_ref[b, h, 2 * nq + j], st_ref[b, h, 2 * nq + n_full])
        gap = gate_q0 - st_ref[b, h, 2 * nq + nk + j]
        dead = reach * 2.0 * k_norm + gap < -UNDERFLOW_BITS
        return jnp.where(jnp.logical_and(dead, first == j), j + 1, first)

    first = lax.fori_loop(0, n_full, skip_scan, jnp.int32(0))
    n_live = n_full - first

    sa_ref[...] = logits(first)

    def full_pair(p, carry):
        j = first + 2 * p
        sb_ref[...] = logits(j + 1)
        step(j, sa_ref[...], masked=False)
        sa_ref[...] = logits(j + 2)
        step(j + 1, sb_ref[...], masked=False)
        return carry

    lax.fori_loop(0, n_live // 2, full_pair, 0)

    @pl.when(n_live % 2 == 1)
    def _():
        sb_ref[...] = logits(n_full)
        step(n_full - 1, sa_ref[...], masked=False)
        step(n_full, sb_ref[...], masked=True)

    @pl.when(n_live % 2 == 0)
    def _():
        step(n_full, sa_ref[...], masked=True)

    acc = acc_ref[...]
    o_ref[...] = (acc[:, :HEAD_DIM] / acc[:, HEAD_DIM:HEAD_DIM + 1]).astype(o_ref.dtype)


def fox_mixer(q_aug, k_aug, v_aug, ck, tq=512, tk=1024):
    B, H, L, _ = q_aug.shape
    tq = min(tq, L)
    tk = min(tk, L)
    assert tk % tq == 0 and L % tk == 0

    norm = lambda a: jnp.sqrt(jnp.sum(jnp.square(a[..., :HEAD_DIM].astype(F32)), axis=-1))
    gate = lambda rows: rows[:, :, GATE_LANE:GATE_LANE + H].transpose(0, 2, 1)
    stats = jnp.concatenate([
        norm(q_aug).reshape(B, H, L // tq, tq).max(axis=-1),
        gate(ck[:, ::tq]),
        norm(k_aug).reshape(B, H, L // tk, tk).max(axis=-1),
        gate(ck[:, tk - 1::tk]),
    ], axis=-1)

    return pl.pallas_call(
        functools.partial(_fox_kernel, tq=tq, tk=tk),
        grid=(B, H, L // tq),
        in_specs=[
            pl.BlockSpec(memory_space=pltpu.SMEM),
            pl.BlockSpec((None, None, tq, FOX_AUG), lambda b, h, i: (b, h, i, 0)),
            pl.BlockSpec((None, None, L, FOX_AUG), lambda b, h, i: (b, h, 0, 0)),
            pl.BlockSpec((None, None, L, FOX_AUG), lambda b, h, i: (b, h, 0, 0)),
        ],
        out_specs=pl.BlockSpec((None, tq, HEAD_DIM), lambda b, h, i: (b, i, h)),
        out_shape=jax.ShapeDtypeStruct((B, L, H * HEAD_DIM), BF16),
        scratch_shapes=[pltpu.VMEM((tq, 1), F32), pltpu.VMEM((tq, FOX_AUG), F32),
                        pltpu.VMEM((tq, tk), F32), pltpu.VMEM((tq, tk), F32)],
        compiler_params=_cparams(("parallel", "parallel", "arbitrary")), name="fox",
    )(stats, q_aug, k_aug, v_aug)


def _out_proj_kernel(x_ref, oa_ref, ob_ref, wa_ref, wb_ref, g_ref, x2_ref, hn_ref):
    y = x_ref[...] + jnp.dot(oa_ref[...], wa_ref[...], preferred_element_type=F32)
    y = y + jnp.dot(ob_ref[...], wb_ref[...], preferred_element_type=F32)
    x2_ref[...] = y
    hn_ref[...] = _rms(y, g_ref[...]).astype(hn_ref.dtype)


def out_proj(x, oa, ob, w_o, g2, tm=512):
    B, L, D = x.shape
    tm = min(tm, L)
    wa = w_o[:N_HEADS_A * HEAD_DIM].astype(BF16)
    wb = w_o[N_HEADS_A * HEAD_DIM:].astype(BF16)
    row = lambda n: pl.BlockSpec((None, tm, n), lambda b, i: (b, i, 0))
    return pl.pallas_call(
        _out_proj_kernel,
        grid=(B, L // tm),
        in_specs=[row(D), row(oa.shape[-1]), row(ob.shape[-1]),
                  _resident(wa.shape, lambda b, i: (0, 0)), _resident(wb.shape, lambda b, i: (0, 0)),
                  pl.BlockSpec((1, D), lambda b, i: (0, 0))],
        out_specs=[row(D), row(D)],
        out_shape=[jax.ShapeDtypeStruct((B, L, D), F32), jax.ShapeDtypeStruct((B, L, D), BF16)],
        compiler_params=_cparams(("parallel", "parallel")), name="out_proj",
    )(x, oa, ob, wa, wb, g2.reshape(1, D))


FFN_HALO = 16


def _ffn_kernel(h_ref, halo_ref, x2_ref, wg_ref, wv_ref, cwg_ref, cwv_ref, cbg_ref, cbv_ref, wd_ref, fg_ref,
                o_ref, hbuf_ref, ug_ref, uv_ref, acc_ref, *, tm):
    i = pl.program_id(1)
    f = pl.program_id(2)

    @pl.when(f == 0)
    def _():
        hbuf_ref[0:FFN_HALO, :] = jnp.where(i > 0, halo_ref[...], jnp.zeros_like(halo_ref))
        hbuf_ref[FFN_HALO:, :] = h_ref[...]
        acc_ref[...] = jnp.zeros(acc_ref.shape, F32)

    ug_ref[...] = jnp.dot(hbuf_ref[...], wg_ref[...], preferred_element_type=F32)
    uv_ref[...] = jnp.dot(hbuf_ref[...], wv_ref[...], preferred_element_type=F32)

    def conv(u_ref, cw_ref, cb_ref):
        a = cb_ref[...] + cw_ref[0:1, :] * u_ref[pl.ds(FFN_HALO - 2, tm), :]
        a = a + cw_ref[1:2, :] * u_ref[pl.ds(FFN_HALO - 1, tm), :]
        return a + cw_ref[2:3, :] * u_ref[pl.ds(FFN_HALO, tm), :]

    gate = conv(ug_ref, cwg_ref, cbg_ref)
    val = conv(uv_ref, cwv_ref, cbv_ref)
    act = (gate * jax.nn.sigmoid(gate) * val).astype(BF16)
    acc_ref[...] += jnp.dot(act, wd_ref[...], preferred_element_type=F32)

    @pl.when(f == pl.num_programs(2) - 1)
    def _():
        o_ref[...] = _rms(x2_ref[...] + acc_ref[...], fg_ref[...]).astype(o_ref.dtype)


def conv_ffn_final(hn, x2, w_up, conv_w, conv_b, w_down, final_g, tm=512, fc=512):
    B, L, D = hn.shape
    F = w_down.shape[0]
    tm = min(tm, L)
    assert F % fc == 0 and tm % FFN_HALO == 0
    nf = F // fc
    wu = w_up.astype(BF16)
    wd = w_down.astype(BF16)
    cb = conv_b.reshape(1, 2 * F)
    hpt = tm // FFN_HALO
    return pl.pallas_call(
        functools.partial(_ffn_kernel, tm=tm),
        grid=(B, L // tm, nf),
        in_specs=[
            pl.BlockSpec((None, tm, D), lambda b, i, f: (b, i, 0)),
            pl.BlockSpec((None, FFN_HALO, D), lambda b, i, f: (b, jnp.maximum(i * hpt - 1, 0), 0)),
            pl.BlockSpec((None, tm, D), lambda b, i, f: (b, i, 0)),
            pl.BlockSpec((D, fc), lambda b, i, f: (0, f)),
            pl.BlockSpec((D, fc), lambda b, i, f: (0, nf + f)),
            pl.BlockSpec((CONV_WIDTH, fc), lambda b, i, f: (0, f)),
            pl.BlockSpec((CONV_WIDTH, fc), lambda b, i, f: (0, nf + f)),
            pl.BlockSpec((1, fc), lambda b, i, f: (0, f)),
            pl.BlockSpec((1, fc), lambda b, i, f: (0, nf + f)),
            pl.BlockSpec((fc, D), lambda b, i, f: (f, 0)),
            pl.BlockSpec((1, D), lambda b, i, f: (0, 0)),
        ],
        out_specs=pl.BlockSpec((None, tm, D), lambda b, i, f: (b, i, 0)),
        out_shape=jax.ShapeDtypeStruct((B, L, D), F32),
        scratch_shapes=[pltpu.VMEM((tm + FFN_HALO, D), BF16),
                        pltpu.VMEM((tm + FFN_HALO, fc), F32),
                        pltpu.VMEM((tm + FFN_HALO, fc), F32),
                        pltpu.VMEM((tm, D), F32)],
        compiler_params=_cparams(("parallel", "parallel", "arbitrary")), name="conv_ffn",
    )(hn, hn, x2, wu, wu, conv_w, conv_w, cb, cb, wd, final_g.reshape(1, D))


def kernel(x, w_in, kv_norm_g, w_uk, w_uv, t5_table, fgate_b, w_o, norm1_g, norm2_g,
           w_up, conv_w, conv_b, w_down, final_g):
    B, L, D = x.shape
    depth = w_in.shape[0]
    na = N_HEADS_A * HEAD_DIM
    nb = N_HEADS_B * HEAD_DIM
    sizes = (na, KV_RANK, IDX_HEADS * IDX_DIM, IDX_DIM, IDX_HEADS, nb, nb, nb, N_HEADS_B)
    offs = np.concatenate([[0], np.cumsum(sizes)])
    assert depth == 1, "the final-norm FFN epilogue closes a single-layer stack"
    assert GATE_LANE + N_HEADS_B <= LANES
    for l in range(depth):
        w = w_in[l]
        w_bq = w[:, offs[5]:offs[6]] * (HEAD_DIM ** -0.5 * LOG2E)
        w_attn = jnp.concatenate([w[:, offs[0]:offs[1]], w_bq, w[:, offs[6]:offs[8]]], axis=1).astype(BF16)
        w_index = jnp.concatenate([w[:, offs[1]:offs[5]], w[:, offs[8]:offs[9]],
                                   jnp.zeros((D, LANES - GATE_LANE - N_HEADS_B), w.dtype)], axis=1).astype(BF16)
        c, ct, qhl, kcat, wt, ck = proj_index(x, norm1_g[l], w_index, kv_norm_g[l], fgate_b[l])
        aq, q_aug, k_aug, v_aug = proj_attn(x, norm1_g[l], w_attn, ck)
        o_a = dsa_mixer(aq, c, ct, qhl, kcat, wt, w_uk[l], w_uv[l], t5_table)
        o_b = fox_mixer(q_aug, k_aug, v_aug, ck)
        x2, hn = out_proj(x, o_a, o_b, w_o[l], norm2_g[l])
        return conv_ffn_final(hn, x2, w_up[l], conv_w[l], conv_b[l], w_down[l], final_g)
```

```python
import functools
import math

import numpy as np
import jax
import jax.numpy as jnp
from jax import lax
from jax.experimental import pallas as pl
from jax.experimental.pallas import tpu as pltpu

HEAD_DIM = 128
N_HEADS_A = 8
N_HEADS_B = 8
KV_RANK = 256
IDX_HEADS = 16
IDX_DIM = 64
TOPK_MAX = 256
N_BUCKETS = 32
MAX_DISTANCE = 128
CONV_WIDTH = 3
EPS = 1e-6

F32 = jnp.float32
BF16 = jnp.bfloat16
INT_MIN = -2 ** 31
NEG = -1e30
LOGIT_SAFE = 60.0
NORM_SLACK = 1.02
UNDERFLOW_BITS = 160.0
LOG2E = math.log2(math.e)

VMEM_LIMIT_BYTES = 56 * 1024 * 1024

DSA_TQ = 128
DSA_TK = 256
IDX_HEAD_GROUP = 4
DSA_SCORE_TILES = 4
DSA_COUNT_TILES = 2
DSA_COUNT_PARTS = 2
DSA_BIT_STAGES = (0, 22, 24, 26, 28, 30, 32)
DSA_FAR_UNROLL = 4


def _cparams(sem):
    return pltpu.CompilerParams(dimension_semantics=sem, vmem_limit_bytes=VMEM_LIMIT_BYTES)


def _resident(block_shape, index_map):
    return pl.BlockSpec(block_shape, index_map, pipeline_mode=pl.Buffered(1))


def _rms(x, g):
    ms = jnp.mean(x * x, axis=-1, keepdims=True)
    return x * lax.rsqrt(ms + EPS) * g


LANES = 128
HALF = LANES // 2
W_IDX_LANE = IDX_DIM
GATE_LANE = IDX_DIM + IDX_HEADS
PROJ_TM = 512


def _bf16_parts(x, parts):
    out = []
    for _ in range(parts):
        p = lax.bitcast_convert_type(lax.bitcast_convert_type(x, jnp.int32) & -65536, F32)
        out.append(p)
        x = x - p
    return out


def _lane_lt(shape, n):
    return lax.broadcasted_iota(jnp.int32, shape, 1) < n


def _proj_index_kernel(x_ref, g_ref, w_ref, kvg_ref, fb_ref,
                       c_ref, ct_ref, qhl_ref, kcat_ref, wt_ref, ck_ref, carry_ref, *, tm):
    i = pl.program_id(1)
    tq = DSA_TQ
    h = _rms(x_ref[...], g_ref[...]).astype(BF16)
    first = _lane_lt((tm, LANES), HALF)

    c = _rms(jnp.dot(h, w_ref[:, 0:KV_RANK], preferred_element_type=F32), kvg_ref[...])
    c_ref[...] = c.astype(BF16)
    ct_ref[...] = c.T.astype(BF16)

    pairs_per_dot = 512 // LANES
    for pair in range(IDX_HEADS // 2):
        if pair % pairs_per_dot == 0:
            c0 = KV_RANK + pair * LANES
            q_wide = jnp.dot(h, w_ref[:, c0:c0 + 512], preferred_element_type=F32)
        u = pair % pairs_per_dot
        q = q_wide[:, u * LANES:(u + 1) * LANES]
        hi, lo = _bf16_parts(q, 2)
        even = jnp.where(first, hi, pltpu.roll(lo, HALF, axis=1)).astype(BF16)
        odd = jnp.where(first, pltpu.roll(hi, HALF, axis=1), lo).astype(BF16)
        for r in range(tm // tq):
            qhl_ref[r, (2 * pair) * tq:(2 * pair + 1) * tq, :] = even[r * tq:(r + 1) * tq, :]
            qhl_ref[r, (2 * pair + 1) * tq:(2 * pair + 2) * tq, :] = odd[r * tq:(r + 1) * tq, :]

    c0 = KV_RANK + IDX_HEADS * IDX_DIM
    y = jnp.dot(h, w_ref[:, c0:c0 + LANES], preferred_element_type=F32)
    hi, lo = _bf16_parts(y, 2)
    kcat_ref[:, 0:LANES] = jnp.where(first, hi, pltpu.roll(hi, HALF, axis=1)).astype(BF16)
    kcat_ref[:, LANES:2 * LANES] = jnp.where(first, lo, pltpu.roll(lo, HALF, axis=1)).astype(BF16)
    for r in range(tm // tq):
        y_t = y[r * tq:(r + 1) * tq, :].T
        wt_ref[r] = y_t[W_IDX_LANE:W_IDX_LANE + IDX_HEADS, :] * (IDX_DIM ** -0.5 * IDX_HEADS ** -0.5)

    z = y + fb_ref[...]
    log_f = (jnp.minimum(z, 0.0) - jnp.log(1.0 + jnp.exp(-jnp.abs(z)))) * LOG2E
    tri = jnp.where(lax.broadcasted_iota(jnp.int32, (tm, tm), 0)
                    >= lax.broadcasted_iota(jnp.int32, (tm, tm), 1), 1.0, 0.0).astype(BF16)
    run = jnp.zeros((tm, LANES), F32)
    for part in _bf16_parts(log_f, CK_PARTS):
        run = run + jnp.dot(tri, part.astype(BF16), preferred_element_type=F32)

    @pl.when(i == 0)
    def _():
        carry_ref[...] = jnp.zeros(carry_ref.shape, F32)

    ck = run + carry_ref[...]
    ck_ref[...] = ck
    carry_ref[...] = ck[tm - 1:tm, :]


def proj_index(x, g, w, kv_g, fgate_b):
    B, L, D = x.shape
    tm, tq = min(PROJ_TM, L), DSA_TQ
    n_cols = KV_RANK + IDX_HEADS * IDX_DIM + LANES
    assert w.shape == (D, n_cols) and tm % tq == 0
    nq = L // tq
    fb = jnp.zeros((1, LANES), F32).at[0, GATE_LANE:GATE_LANE + N_HEADS_B].set(fgate_b.astype(F32))
    row = lambda n, dt: (pl.BlockSpec((None, tm, n), lambda b, i: (b, i, 0)), jax.ShapeDtypeStruct((B, L, n), dt))
    outs = [
        row(KV_RANK, BF16),
        (pl.BlockSpec((None, KV_RANK, tm), lambda b, i: (b, 0, i)), jax.ShapeDtypeStruct((B, KV_RANK, L), BF16)),
        (pl.BlockSpec((None, tm // tq, IDX_HEADS * tq, 2 * IDX_DIM), lambda b, i: (b, i, 0, 0)),
         jax.ShapeDtypeStruct((B, nq, IDX_HEADS * tq, 2 * IDX_DIM), BF16)),
        row(4 * IDX_DIM, BF16),
        (pl.BlockSpec((None, tm // tq, IDX_HEADS, tq), lambda b, i: (b, i, 0, 0)),
         jax.ShapeDtypeStruct((B, nq, IDX_HEADS, tq), F32)),
        row(LANES, F32),
    ]
    return pl.pallas_call(
        functools.partial(_proj_index_kernel, tm=tm),
        grid=(B, L // tm),
        in_specs=[pl.BlockSpec((None, tm, D), lambda b, i: (b, i, 0)),
                  pl.BlockSpec((1, D), lambda b, i: (0, 0)),
                  _resident((D, n_cols), lambda b, i: (0, 0)),
                  pl.BlockSpec((1, KV_RANK), lambda b, i: (0, 0)),
                  pl.BlockSpec((1, LANES), lambda b, i: (0, 0))],
        out_specs=[o[0] for o in outs], out_shape=[o[1] for o in outs],
        scratch_shapes=[pltpu.VMEM((1, LANES), F32)],
        compiler_params=_cparams(("parallel", "arbitrary")), name="proj_index",
    )(x, g.reshape(1, D), w, kv_g.reshape(1, KV_RANK), fb)


def _proj_attn_kernel(x_ref, g_ref, w_ref, ck_ref, aq_ref, qa_ref, ka_ref, va_ref, *, tm):
    h = _rms(x_ref[...], g_ref[...]).astype(BF16)
    na = N_HEADS_A * HEAD_DIM
    for c0 in range(0, na, 512):
        aq_ref[:, c0:c0 + 512] = jnp.dot(h, w_ref[:, c0:c0 + 512], preferred_element_type=F32).astype(BF16)

    lane = lax.broadcasted_iota(jnp.int32, (tm, LANES), 1)
    qa_tail = jnp.where(lane < CK_PARTS, -1.0, 0.0).astype(BF16)
    va_tail = jnp.where(lane < 1, 1.0, 0.0).astype(BF16)
    parts = _bf16_parts(ck_ref[...], CK_PARTS)
    group = 512 // HEAD_DIM
    for dst, base in ((qa_ref, na), (ka_ref, na + N_HEADS_B * HEAD_DIM), (va_ref, na + 2 * N_HEADS_B * HEAD_DIM)):
        for g0 in range(0, N_HEADS_B, group):
            c0 = base + g0 * HEAD_DIM
            y = jnp.dot(h, w_ref[:, c0:c0 + 512], preferred_element_type=F32).astype(BF16)
            for u in range(group):
                dst[g0 + u, :, 0:HEAD_DIM] = y[:, u * HEAD_DIM:(u + 1) * HEAD_DIM]
    for hd in range(N_HEADS_B):
        qa_ref[hd, :, HEAD_DIM:] = qa_tail
        va_ref[hd, :, HEAD_DIM:] = va_tail
        tail = jnp.zeros((tm, LANES), F32)
        for j, p in enumerate(parts):
            moved = pltpu.roll(p, (LANES - GATE_LANE - hd + j) % LANES, axis=1)
            tail = jnp.where(lane == j, moved, tail)
        ka_ref[hd, :, HEAD_DIM:] = tail.astype(BF16)


def proj_attn(x, g, w, ck):
    B, L, D = x.shape
    tm = min(PROJ_TM, L)
    na, H = N_HEADS_A * HEAD_DIM, N_HEADS_B
    aug = (pl.BlockSpec((None, H, tm, FOX_AUG), lambda b, i: (b, 0, i, 0)),
           jax.ShapeDtypeStruct((B, H, L, FOX_AUG), BF16))
    outs = [(pl.BlockSpec((None, tm, na), lambda b, i: (b, i, 0)), jax.ShapeDtypeStruct((B, L, na), BF16)),
            aug, aug, aug]
    return pl.pallas_call(
        functools.partial(_proj_attn_kernel, tm=tm),
        grid=(B, L // tm),
        in_specs=[pl.BlockSpec((None, tm, D), lambda b, i: (b, i, 0)),
                  pl.BlockSpec((1, D), lambda b, i: (0, 0)),
                  _resident(w.shape, lambda b, i: (0, 0)),
                  pl.BlockSpec((None, tm, LANES), lambda b, i: (b, i, 0))],
        out_specs=[o[0] for o in outs], out_shape=[o[1] for o in outs],
        compiler_params=_cparams(("parallel", "parallel")), name="proj_attn",
    )(x, g.reshape(1, D), w, ck)


def _dsa_kernel(bnd_ref, aq_ref, wuk_ref, qhl_ref, wt_ref, kcat_ref, c_ref, ct_ref, bias_ref, wuv_ref,
                out_ref, s_ref, sel_ref, qcat_ref, ql_ref, acc_ref, m_ref, l_ref, lta_ref, ltb_ref, *, topk):
    tq, tk = DSA_TQ, DSA_TK
    i = pl.program_id(1)
    nkt = (i * tq) // tk + 1
    t_pos = i * tq + lax.broadcasted_iota(jnp.int32, (tk, tq), 1)
    s_loc = lax.broadcasted_iota(jnp.int32, (tk, tq), 0)

    scale = HEAD_DIM ** -0.5 * LOG2E
    qn2 = jnp.float32(0.0)
    for h in range(N_HEADS_A):
        ql = jnp.dot(aq_ref[:, h * HEAD_DIM:(h + 1) * HEAD_DIM], wuk_ref[h], preferred_element_type=F32)
        ql = ql * scale
        qn2 = jnp.maximum(qn2, jnp.max(jnp.sum(ql * ql, axis=1, keepdims=True)))
        ql_ref[h * tq:(h + 1) * tq, :] = ql.astype(BF16)

    qcat_ref[...] = jnp.concatenate([qhl_ref[...], qhl_ref[...]], axis=1)

    sub = DSA_SCORE_TILES
    tc = sub * tk
    n_chunks = (i * tq) // tc + 1

    def score_chunk(jc, carry):
        for u in range(sub):
            k0 = pl.multiple_of(jc * tc + u * tk, tk)
            kcat = kcat_ref[pl.ds(k0, tk), :]
            s = jnp.zeros((tk, tq), F32)
            hg = IDX_HEAD_GROUP
            for g in range(IDX_HEADS // hg):
                d = lax.dot_general(kcat, qcat_ref[g * hg * tq:(g + 1) * hg * tq, :],
                                    (((1,), (1,)), ((), ())), preferred_element_type=F32)
                for hh in range(hg):
                    h = g * hg + hh
                    s = s + jnp.maximum(d[:, hh * tq:(hh + 1) * tq], 0.0) * wt_ref[h:h + 1, :]
            s = jnp.where(s == 0.0, 0.0, s)
            bits = lax.bitcast_convert_type(s, jnp.int32)
            key = bits ^ ((bits >> 31) & 0x7FFFFFFF)
            key = jnp.where(k0 + s_loc <= t_pos, key, INT_MIN)
            s_ref[pl.ds(k0, tk), :] = key
        return carry

    lax.fori_loop(0, n_chunks, score_chunk, 0)

    part = DSA_COUNT_TILES * tk
    rows = DSA_COUNT_PARTS * part
    n_count = (i * tq) // rows + 1

    def count_ge(cand):
        def body(jc, acc):
            for u in range(DSA_COUNT_PARTS):
                k0 = pl.multiple_of(jc * rows + u * part, part)
                ge = jnp.where(s_ref[pl.ds(k0, part), :] >= cand, 1, 0).astype(jnp.int32)
                acc = acc + ge.reshape(part // 8, 8, tq).sum(axis=0)
            return acc
        acc = lax.fori_loop(0, n_count, body, jnp.zeros((8, tq), jnp.int32))
        return acc.sum(axis=0, keepdims=True)

    def bit_step(b, carry):
        t, cnt_t = carry
        cand = t + lax.shift_left(jnp.int32(1), 31 - b)
        cnt = count_ge(cand)
        take = cnt >= topk
        return jnp.where(take, cand, t), jnp.where(take, cnt, cnt_t)

    sel_ref[0:1, :] = jnp.full((1, tq), INT_MIN, jnp.int32)
    sel_ref[1:2, :] = jnp.zeros((1, tq), jnp.int32) + n_count * rows
    def bit_stage(b0, b1):
        t, cnt_t = lax.fori_loop(b0, b1, bit_step, (sel_ref[0:1, :], sel_ref[1:2, :]))
        sel_ref[0:1, :] = t
        sel_ref[1:2, :] = cnt_t

    bit_stage(DSA_BIT_STAGES[0], DSA_BIT_STAGES[1])
    for b0, b1 in zip(DSA_BIT_STAGES[1:-1], DSA_BIT_STAGES[2:]):
        pl.when(jnp.max(sel_ref[1:2, :]) > topk)(functools.partial(bit_stage, b0, b1))
    thr, cnt_thr = sel_ref[0:1, :], sel_ref[1:2, :]

    @pl.when(jnp.max(cnt_thr) > topk)
    def _():
        need = (topk - count_ge(thr + 1)).astype(F32)
        tri = jnp.where(lax.broadcasted_iota(jnp.int32, (tk, tk), 0)
                        > lax.broadcasted_iota(jnp.int32, (tk, tk), 1), 1.0, 0.0).astype(BF16)

        def body(j, seen):
            k0 = pl.multiple_of(j * tk, tk)
            x = s_ref[pl.ds(k0, tk), :]
            eq = x == thr
            eqf = jnp.where(eq, 1.0, 0.0)
            rank = jnp.dot(tri, eqf.astype(BF16), preferred_element_type=F32) + seen
            s_ref[pl.ds(k0, tk), :] = jnp.where(eq & (rank >= need), INT_MIN, x)
            return seen + eqf.sum(axis=0, keepdims=True)

        lax.fori_loop(0, nkt, body, jnp.zeros((1, tq), F32))

    def logits(j):
        k0 = pl.multiple_of(j * tk, tk)
        return lax.dot_general(c_ref[pl.ds(k0, tk), :], ql_ref[...],
                               (((1,), (1,)), ((), ())), preferred_element_type=F32)

    def attend_tile(j, lt, near, bounded):
        k0 = pl.multiple_of(j * tk, tk)
        sel = s_ref[pl.ds(k0, tk), :] >= thr
        if near:
            sel = sel & (k0 + s_loc <= t_pos)
            variant = jnp.minimum((i * tq - k0) // MAX_DISTANCE, 2)
            lt = lt + bias_ref[variant]
        ps, ms, sums = [], [], []
        if not bounded:
            m_old = m_ref[...]
        for h in range(N_HEADS_A):
            z = jnp.where(sel, lt[:, h * tq:(h + 1) * tq], NEG)
            if not bounded:
                m_new = jnp.maximum(m_old[:, h * tq:(h + 1) * tq], z.max(axis=0, keepdims=True))
                ms.append(m_new)
                z = z - m_new
            p = jnp.exp2(z)
            sums.append(p.sum(axis=0, keepdims=True))
            ps.append(p.astype(BF16))
        pv = jnp.dot(ct_ref[:, pl.ds(k0, tk)], jnp.concatenate(ps, axis=1), preferred_element_type=F32)
        if bounded:
            l_ref[...] += jnp.concatenate(sums, axis=1)
            acc_ref[...] += pv
        else:
            m_new = jnp.concatenate(ms, axis=1)
            alpha = jnp.exp2(m_old - m_new)
            l_ref[...] = alpha * l_ref[...] + jnp.concatenate(sums, axis=1)
            m_ref[...] = m_new
            acc_ref[...] = alpha * acc_ref[...] + pv

    n_far = jnp.maximum(i * tq - (tk + MAX_DISTANCE) + tk, 0) // tk

    def attend_all(bounded):
        m_ref[...] = jnp.full(m_ref.shape, NEG, F32)
        l_ref[...] = jnp.zeros(l_ref.shape, F32)
        acc_ref[...] = jnp.zeros(acc_ref.shape, F32)
        lta_ref[...] = logits(0)

        def far_group(p, carry, base, width):
            for u in range(0, width, 2):
                j = base + width * p + u
                ltb_ref[...] = logits(j + 1)
                attend_tile(j, lta_ref[...], False, bounded)
                lta_ref[...] = logits(j + 2)
                attend_tile(j + 1, ltb_ref[...], False, bounded)
            return carry

        wide = DSA_FAR_UNROLL
        n_wide = n_far // wide
        lax.fori_loop(0, n_wide, functools.partial(far_group, base=0, width=wide), 0)
        lax.fori_loop(0, (n_far - wide * n_wide) // 2,
                      functools.partial(far_group, base=wide * n_wide, width=2), 0)

        @pl.when(n_far % 2 == 1)
        def _():
            attend_tile(n_far - 1, lta_ref[...], False, bounded)

        def near_tile(j, carry):
            attend_tile(j, logits(j), True, bounded)
            return carry

        lax.fori_loop(n_far, nkt, near_tile, 0)

    logit_bound = jnp.sqrt(qn2) * bnd_ref[pl.program_id(0), 0] * NORM_SLACK + bnd_ref[pl.program_id(0), 1]
    safe = logit_bound <= LOGIT_SAFE

    @pl.when(safe)
    def _():
        attend_all(bounded=True)

    @pl.when(jnp.logical_not(safe))
    def _():
        attend_all(bounded=False)

    o_t = acc_ref[...] / l_ref[...]
    for h in range(N_HEADS_A):
        o_h = o_t[:, h * tq:(h + 1) * tq].T.astype(BF16)
        out_ref[:, h * HEAD_DIM:(h + 1) * HEAD_DIM] = jnp.dot(
            o_h, wuv_ref[h], preferred_element_type=F32).astype(out_ref.dtype)


def _t5_bucket_np(n):
    max_exact = N_BUCKETS // 2
    nf = np.maximum(n, 1).astype(np.float32)
    large = max_exact + (np.log(nf / max_exact) / math.log(MAX_DISTANCE / max_exact)
                         * (N_BUCKETS - max_exact)).astype(np.int32)
    large = np.minimum(large, N_BUCKETS - 1)
    return np.where(n < max_exact, n, large)


def _dsa_bias_tiles(t5_table):
    tq, tk = DSA_TQ, DSA_TK
    assert MAX_DISTANCE == 128 and tq == 128 and tk == 256
    sl = np.arange(tk)[:, None]
    tl = np.arange(tq)[None, :]
    idx = np.stack([_t5_bucket_np(np.maximum(tl - sl + off, 0)) for off in (0, 128, 256)])
    table = (t5_table.astype(F32) - t5_table[N_BUCKETS - 1].astype(F32)) * LOG2E
    onehot = (jnp.asarray(idx)[..., None] == jnp.arange(N_BUCKETS)).astype(F32)
    near = jnp.einsum('vstn,nh->vsht', onehot, table, precision=lax.Precision.HIGHEST)
    return near.reshape(3, tk, N_HEADS_A * tq), jnp.max(jnp.abs(table))


def dsa_mixer(aq, c, ct, qhl, kcat, wt, w_uk, w_uv, t5_table):
    B, L, _ = aq.shape
    tq, tk = DSA_TQ, DSA_TK
    assert L % (DSA_SCORE_TILES * tk) == 0
    nq = L // tq
    topk = min(TOPK_MAX, L // 4)
    assert tk >= topk
    bias, bias_max = _dsa_bias_tiles(t5_table)
    c_norm = jnp.sqrt(jnp.max(jnp.sum(jnp.square(c.astype(F32)), axis=-1), axis=-1))
    bounds = jnp.stack([c_norm, jnp.broadcast_to(bias_max, (B,))], axis=-1)

    hq = N_HEADS_A * tq
    return pl.pallas_call(
        functools.partial(_dsa_kernel, topk=topk),
        grid=(B, nq),
        in_specs=[
            pl.BlockSpec(memory_space=pltpu.SMEM),
            pl.BlockSpec((None, tq, N_HEADS_A * HEAD_DIM), lambda b, i: (b, i, 0)),
            _resident((N_HEADS_A, HEAD_DIM, KV_RANK), lambda b, i: (0, 0, 0)),
            pl.BlockSpec((None, None, IDX_HEADS * tq, 2 * IDX_DIM), lambda b, i: (b, i, 0, 0)),
            pl.BlockSpec((None, None, IDX_HEADS, tq), lambda b, i: (b, i, 0, 0)),
            _resident((None, L, 4 * IDX_DIM), lambda b, i: (b, 0, 0)),
            _resident((None, L, KV_RANK), lambda b, i: (b, 0, 0)),
            _resident((None, KV_RANK, L), lambda b, i: (b, 0, 0)),
            _resident((3, tk, hq), lambda b, i: (0, 0, 0)),
            _resident((N_HEADS_A, KV_RANK, HEAD_DIM), lambda b, i: (0, 0, 0)),
        ],
        out_specs=pl.BlockSpec((None, tq, N_HEADS_A * HEAD_DIM), lambda b, i: (b, i, 0)),
        out_shape=jax.ShapeDtypeStruct((B, L, N_HEADS_A * HEAD_DIM), BF16),
        scratch_shapes=[
            pltpu.VMEM((L, tq), jnp.int32),
            pltpu.VMEM((2, tq), jnp.int32),
            pltpu.VMEM((IDX_HEADS * tq, 4 * IDX_DIM), BF16),
            pltpu.VMEM((hq, KV_RANK), BF16),
            pltpu.VMEM((KV_RANK, hq), F32),
            pltpu.VMEM((1, hq), F32),
            pltpu.VMEM((1, hq), F32),
            pltpu.VMEM((tk, hq), F32),
            pltpu.VMEM((tk, hq), F32),
        ],
        compiler_params=_cparams(("parallel", "arbitrary")), name="dsa",
    )(bounds, aq, w_uk.astype(BF16), qhl, wt, kcat, c, ct, bias, w_uv.astype(BF16))


FOX_AUG = 2 * HEAD_DIM
CK_PARTS = 3


def _fox_kernel(st_ref, q_ref, k_ref, v_ref, o_ref, m_ref, acc_ref, sa_ref, sb_ref, *, tq, tk):
    i = pl.program_id(2)
    n_full = (i * tq) // tk
    m_ref[...] = jnp.full(m_ref.shape, NEG, F32)
    acc_ref[...] = jnp.zeros(acc_ref.shape, F32)

    def logits(j):
        k0 = pl.multiple_of(j * tk, tk)
        return lax.dot_general(q_ref[...], k_ref[pl.ds(k0, tk), :], (((1,), (1,)), ((), ())),
                               preferred_element_type=F32)

    def step(j, s, masked):
        k0 = pl.multiple_of(j * tk, tk)
        if masked:
            t_pos = i * tq + lax.broadcasted_iota(jnp.int32, (tq, tk), 0)
            s_pos = k0 + lax.broadcasted_iota(jnp.int32, (tq, tk), 1)
            s = jnp.where(s_pos <= t_pos, s, NEG)
        m_old = m_ref[...]
        m_new = jnp.maximum(m_old, s.max(axis=1, keepdims=True))
        p = jnp.exp2(s - m_new).astype(BF16)
        pv = jnp.dot(p, v_ref[pl.ds(k0, tk), :], preferred_element_type=F32)
        acc_ref[...] = jnp.exp2(m_old - m_new) * acc_ref[...] + pv
        m_ref[...] = m_new

    b, h = pl.program_id(0), pl.program_id(1)
    nq, nk = pl.num_programs(2), k_ref.shape[0] // tk
    q_norm = st_ref[b, h, i]
    gate_q0 = st_ref[b, h, nq + i]
    reach = q_norm * NORM_SLACK

    def skip_scan(j, first):
        k_norm = jnp.maximum(st_ref[b, h, 2 * nq + j], st_ref[b, h, 2 * nq + n_full])
        gap = gate_q0 - st_ref[b, h, 2 * nq + nk + j]
        dead = reach * 2.0 * k_norm + gap < -UNDERFLOW_BITS
        return jnp.where(jnp.logical_and(dead, first == j), j + 1, first)

    first = lax.fori_loop(0, n_full, skip_scan, jnp.int32(0))
    n_live = n_full - first

    sa_ref[...] = logits(first)

    def full_pair(p, carry):
        j = first + 2 * p
        sb_ref[...] = logits(j + 1)
        step(j, sa_ref[...], masked=False)
        sa_ref[...] = logits(j + 2)
        step(j + 1, sb_ref[...], masked=False)
        return carry

    lax.fori_loop(0, n_live // 2, full_pair, 0)

    @pl.when(n_live % 2 == 1)
    def _():
        sb_ref[...] = logits(n_full)
        step(n_full - 1, sa_ref[...], masked=False)
        step(n_full, sb_ref[...], masked=True)

    @pl.when(n_live % 2 == 0)
    def _():
        step(n_full, sa_ref[...], masked=True)

    acc = acc_ref[...]
    o_ref[...] = (acc[:, :HEAD_DIM] / acc[:, HEAD_DIM:HEAD_DIM + 1]).astype(o_ref.dtype)


def fox_mixer(q_aug, k_aug, v_aug, ck, tq=512, tk=1024):
    B, H, L, _ = q_aug.shape
    tq = min(tq, L)
    tk = min(tk, L)
    assert tk % tq == 0 and L % tk == 0

    norm = lambda a: jnp.sqrt(jnp.sum(jnp.square(a[..., :HEAD_DIM].astype(F32)), axis=-1))
    gate = lambda rows: rows[:, :, GATE_LANE:GATE_LANE + H].transpose(0, 2, 1)
    stats = jnp.concatenate([
        norm(q_aug).reshape(B, H, L // tq, tq).max(axis=-1),
        gate(ck[:, ::tq]),
        norm(k_aug).reshape(B, H, L // tk, tk).max(axis=-1),
        gate(ck[:, tk - 1::tk]),
    ], axis=-1)

    return pl.pallas_call(
        functools.partial(_fox_kernel, tq=tq, tk=tk),
        grid=(B, H, L // tq),
        in_specs=[
            pl.BlockSpec(memory_space=pltpu.SMEM),
            pl.BlockSpec((None, None, tq, FOX_AUG), lambda b, h, i: (b, h, i, 0)),
            pl.BlockSpec((None, None, L, FOX_AUG), lambda b, h, i: (b, h, 0, 0)),
            pl.BlockSpec((None, None, L, FOX_AUG), lambda b, h, i: (b, h, 0, 0)),
        ],
        out_specs=pl.BlockSpec((None, tq, HEAD_DIM), lambda b, h, i: (b, i, h)),
        out_shape=jax.ShapeDtypeStruct((B, L, H * HEAD_DIM), BF16),
        scratch_shapes=[pltpu.VMEM((tq, 1), F32), pltpu.VMEM((tq, FOX_AUG), F32),
                        pltpu.VMEM((tq, tk), F32), pltpu.VMEM((tq, tk), F32)],
        compiler_params=_cparams(("parallel", "parallel", "arbitrary")), name="fox",
    )(stats, q_aug, k_aug, v_aug)


def _out_proj_kernel(x_ref, oa_ref, ob_ref, wa_ref, wb_ref, g_ref, x2_ref, hn_ref):
    y = x_ref[...] + jnp.dot(oa_ref[...], wa_ref[...], preferred_element_type=F32)
    y = y + jnp.dot(ob_ref[...], wb_ref[...], preferred_element_type=F32)
    x2_ref[...] = y
    hn_ref[...] = _rms(y, g_ref[...]).astype(hn_ref.dtype)


def out_proj(x, oa, ob, w_o, g2, tm=512):
    B, L, D = x.shape
    tm = min(tm, L)
    wa = w_o[:N_HEADS_A * HEAD_DIM].astype(BF16)
    wb = w_o[N_HEADS_A * HEAD_DIM:].astype(BF16)
    row = lambda n: pl.BlockSpec((None, tm, n), lambda b, i: (b, i, 0))
    return pl.pallas_call(
        _out_proj_kernel,
        grid=(B, L // tm),
        in_specs=[row(D), row(oa.shape[-1]), row(ob.shape[-1]),
                  _resident(wa.shape, lambda b, i: (0, 0)), _resident(wb.shape, lambda b, i: (0, 0)),
                  pl.BlockSpec((1, D), lambda b, i: (0, 0))],
        out_specs=[row(D), row(D)],
        out_shape=[jax.ShapeDtypeStruct((B, L, D), F32), jax.ShapeDtypeStruct((B, L, D), BF16)],
        compiler_params=_cparams(("parallel", "parallel")), name="out_proj",
    )(x, oa, ob, wa, wb, g2.reshape(1, D))


FFN_HALO = 16


def _ffn_kernel(h_ref, halo_ref, x2_ref, wg_ref, wv_ref, cwg_ref, cwv_ref, cbg_ref, cbv_ref, wd_ref, fg_ref,
                o_ref, hbuf_ref, ug_ref, uv_ref, acc_ref, *, tm):
    i = pl.program_id(1)
    f = pl.program_id(2)

    @pl.when(f == 0)
    def _():
        hbuf_ref[0:FFN_HALO, :] = jnp.where(i > 0, halo_ref[...], jnp.zeros_like(halo_ref))
        hbuf_ref[FFN_HALO:, :] = h_ref[...]
        acc_ref[...] = jnp.zeros(acc_ref.shape, F32)

    ug_ref[...] = jnp.dot(hbuf_ref[...], wg_ref[...], preferred_element_type=F32)
    uv_ref[...] = jnp.dot(hbuf_ref[...], wv_ref[...], preferred_element_type=F32)

    def conv(u_ref, cw_ref, cb_ref):
        a = cb_ref[...] + cw_ref[0:1, :] * u_ref[pl.ds(FFN_HALO - 2, tm), :]
        a = a + cw_ref[1:2, :] * u_ref[pl.ds(FFN_HALO - 1, tm), :]
        return a + cw_ref[2:3, :] * u_ref[pl.ds(FFN_HALO, tm), :]

    gate = conv(ug_ref, cwg_ref, cbg_ref)
    val = conv(uv_ref, cwv_ref, cbv_ref)
    act = (gate * jax.nn.sigmoid(gate) * val).astype(BF16)
    acc_ref[...] += jnp.dot(act, wd_ref[...], preferred_element_type=F32)

    @pl.when(f == pl.num_programs(2) - 1)
    def _():
        o_ref[...] = _rms(x2_ref[...] + acc_ref[...], fg_ref[...]).astype(o_ref.dtype)


def conv_ffn_final(hn, x2, w_up, conv_w, conv_b, w_down, final_g, tm=512, fc=512):
    B, L, D = hn.shape
    F = w_down.shape[0]
    tm = min(tm, L)
    assert F % fc == 0 and tm % FFN_HALO == 0
    nf = F // fc
    wu = w_up.astype(BF16)
    wd = w_down.astype(BF16)
    cb = conv_b.reshape(1, 2 * F)
    hpt = tm // FFN_HALO
    return pl.pallas_call(
        functools.partial(_ffn_kernel, tm=tm),
        grid=(B, L // tm, nf),
        in_specs=[
            pl.BlockSpec((None, tm, D), lambda b, i, f: (b, i, 0)),
            pl.BlockSpec((None, FFN_HALO, D), lambda b, i, f: (b, jnp.maximum(i * hpt - 1, 0), 0)),
            pl.BlockSpec((None, tm, D), lambda b, i, f: (b, i, 0)),
            pl.BlockSpec((D, fc), lambda b, i, f: (0, f)),
            pl.BlockSpec((D, fc), lambda b, i, f: (0, nf + f)),
            pl.BlockSpec((CONV_WIDTH, fc), lambda b, i, f: (0, f)),
            pl.BlockSpec((CONV_WIDTH, fc), lambda b, i, f: (0, nf + f)),
            pl.BlockSpec((1, fc), lambda b, i, f: (0, f)),
            pl.BlockSpec((1, fc), lambda b, i, f: (0, nf + f)),
            pl.BlockSpec((fc, D), lambda b, i, f: (f, 0)),
            pl.BlockSpec((1, D), lambda b, i, f: (0, 0)),
        ],
        out_specs=pl.BlockSpec((None, tm, D), lambda b, i, f: (b, i, 0)),
        out_shape=jax.ShapeDtypeStruct((B, L, D), F32),
        scratch_shapes=[pltpu.VMEM((tm + FFN_HALO, D), BF16),
                        pltpu.VMEM((tm + FFN_HALO, fc), F32),
                        pltpu.VMEM((tm + FFN_HALO, fc), F32),
                        pltpu.VMEM((tm, D), F32)],
        compiler_params=_cparams(("parallel", "parallel", "arbitrary")), name="conv_ffn",
    )(hn, hn, x2, wu, wu, conv_w, conv_w, cb, cb, wd, final_g.reshape(1, D))


def kernel(x, w_in, kv_norm_g, w_uk, w_uv, t5_table, fgate_b, w_o, norm1_g, norm2_g,
           w_up, conv_w, conv_b, w_down, final_g):
    B, L, D = x.shape
    depth = w_in.shape[0]
    na = N_HEADS_A * HEAD_DIM
    nb = N_HEADS_B * HEAD_DIM
    sizes = (na, KV_RANK, IDX_HEADS * IDX_DIM, IDX_DIM, IDX_HEADS, nb, nb, nb, N_HEADS_B)
    offs = np.concatenate([[0], np.cumsum(sizes)])
    assert depth == 1, "the final-norm FFN epilogue closes a single-layer stack"
    assert GATE_LANE + N_HEADS_B <= LANES
    for l in range(depth):
        w = w_in[l]
        w_bq = w[:, offs[5]:offs[6]] * (HEAD_DIM ** -0.5 * LOG2E)
        w_attn = jnp.concatenate([w[:, offs[0]:offs[1]], w_bq, w[:, offs[6]:offs[8]]], axis=1).astype(BF16)
        w_index = jnp.concatenate([w[:, offs[1]:offs[5]], w[:, offs[8]:offs[9]],
                                   jnp.zeros((D, LANES - GATE_LANE - N_HEADS_B), w.dtype)], axis=1).astype(BF16)
        c, ct, qhl, kcat, wt, ck = proj_index(x, norm1_g[l], w_index, kv_norm_g[l], fgate_b[l])
        aq, q_aug, k_aug, v_aug = proj_attn(x, norm1_g[l], w_attn, ck)
        o_a = dsa_mixer(aq, c, ct, qhl, kcat, wt, w_uk[l], w_uv[l], t5_table)
        o_b = fox_mixer(q_aug, k_aug, v_aug, ck)
        x2, hn = out_proj(x, o_a, o_b, w_o[l], norm2_g[l])
        return conv_ffn_final(hn, x2, w_up[l], conv_w[l], conv_b[l], w_down[l], final_g)
```
